```python
import math
import jax
import jax.numpy as jnp
from jax import lax
import numpy as np

D_MODEL = 1024
BATCH = 2
SEQ = 16384
DEPTH = 4

HEAD_DIM = 64
N_MIXERS = 4
GROUP_WIDTH = D_MODEL // N_MIXERS
MIX_WIDTH = N_MIXERS * GROUP_WIDTH
ROPE_THETA = 500000.0
ROPE_FRACTION = 4
Q_BLOCK = 128
NEG_INF = -1e30

DIFF_HEADS = GROUP_WIDTH // HEAD_DIM
DIFF_QK_DIM = HEAD_DIM // 2

DIL_HEADS = GROUP_WIDTH // HEAD_DIM
DIL_PATTERNS = ((128, 1), (512, 4), (2048, 16))

POOL_WINDOWS = (2, 4, 8, 16)
POOL_GROUP = GROUP_WIDTH // len(POOL_WINDOWS)

MLA_HEADS = GROUP_WIDTH // HEAD_DIM
MLA_Q_RANK = GROUP_WIDTH
MLA_KV_RANK = D_MODEL // 8
MLA_NOPE_DIM = HEAD_DIM
MLA_ROPE_DIM = HEAD_DIM // 2
MLA_V_DIM = HEAD_DIM

A_COLS = 3 * GROUP_WIDTH
B_COLS = 3 * GROUP_WIDTH
C_COLS = GROUP_WIDTH
D_COLS = MLA_Q_RANK + MLA_KV_RANK + MLA_ROPE_DIM
IN_COLS = A_COLS + B_COLS + C_COLS + D_COLS

D_FF = 2816
CONV_WIDTH = 3

DN_ALPHA = (2 * DEPTH) ** 0.25
DN_BETA = (8 * DEPTH) ** -0.25
LN_EPS = 1e-5
RMS_EPS = 1e-6

kernel_name = "hybrid_parallel_head_encoder"


def _layer_norm(x, g, b):
    xf = x.astype(jnp.float32)
    mu = jnp.mean(xf, axis=-1, keepdims=True)
    xc = xf - mu
    var = jnp.mean(xc * xc, axis=-1, keepdims=True)
    return (xc * lax.rsqrt(var + LN_EPS) * g + b).astype(x.dtype)


def _rms_norm(x, g):
    xf = x.astype(jnp.float32)
    y = xf * lax.rsqrt(jnp.mean(xf * xf, axis=-1, keepdims=True) + RMS_EPS)
    return (y * g).astype(x.dtype)


def _rope(x, positions, rot_dim):
    half = rot_dim // 2
    inv_freq = ROPE_THETA ** (-jnp.arange(half, dtype=jnp.float32) * 2.0 / rot_dim)
    ang = positions.astype(jnp.float32)[..., None] * inv_freq
    cos = jnp.cos(ang)[:, :, None, :]
    sin = jnp.sin(ang)[:, :, None, :]
    xr = x[..., :rot_dim].astype(jnp.float32)
    x1, x2 = xr[..., :half], xr[..., half:]
    rot = jnp.concatenate([x1 * cos - x2 * sin, x1 * sin + x2 * cos], axis=-1).astype(x.dtype)
    return jnp.concatenate([rot, x[..., rot_dim:]], axis=-1)


def _to_qblocks(t):
    b, s, h, d = t.shape
    return t.reshape(b, s // Q_BLOCK, Q_BLOCK, h, d).swapaxes(0, 1)


def _from_qblocks(t):
    nb, b, qb, h, d = t.shape
    return t.swapaxes(0, 1).reshape(b, nb * qb, h, d)


def _dense_attention(q, k, v, scale):
    def block(qb):
        s = jnp.einsum('bqhd,bkhd->bhqk', qb, k).astype(jnp.float32) * scale
        p = jax.nn.softmax(s, axis=-1)
        return jnp.einsum('bhqk,bkhd->bqhd', p.astype(v.dtype), v)
    return _from_qblocks(lax.map(block, _to_qblocks(q)))


def _diff_attention(q1, k1, q2, k2, v, lam, scale):
    def block(qs):
        qa, qb = qs
        p1 = jax.nn.softmax(jnp.einsum('bqhd,bkhd->bhqk', qa, k1).astype(jnp.float32) * scale, axis=-1)
        p2 = jax.nn.softmax(jnp.einsum('bqhd,bkhd->bhqk', qb, k2).astype(jnp.float32) * scale, axis=-1)
        w = p1 - lam * p2
        return jnp.einsum('bhqk,bkhd->bqhd', w.astype(v.dtype), v)
    return _from_qblocks(lax.map(block, (_to_qblocks(q1), _to_qblocks(q2))))


def _banded_attention(q, k, v, half):
    n, l, h, d = q.shape
    nb = -(-l // Q_BLOCK)
    lp = nb * Q_BLOCK
    span = Q_BLOCK + 2 * half
    qp = jnp.pad(q, ((0, 0), (0, lp - l), (0, 0), (0, 0))).reshape(n, nb, Q_BLOCK, h, d)
    kpad = ((0, 0), (half, half + lp - l), (0, 0), (0, 0))
    kp = jnp.pad(k, kpad)
    vp = jnp.pad(v, kpad)
    idx = jnp.arange(nb)[:, None] * Q_BLOCK + jnp.arange(span)[None, :]
    kb = kp[:, idx]
    vb = vp[:, idx]
    s = jnp.einsum('nbqhd,nbkhd->nbhqk', qp, kb).astype(jnp.float32) * (d ** -0.5)
    key_pos = idx - half
    q_pos = jnp.arange(nb)[:, None] * Q_BLOCK + jnp.arange(Q_BLOCK)[None, :]
    rel = key_pos[:, None, :] - q_pos[:, :, None]
    valid = (jnp.abs(rel) <= half) & (key_pos[:, None, :] >= 0) & (key_pos[:, None, :] < l)
    s = jnp.where(valid[None, :, None, :, :], s, NEG_INF)
    m = jnp.max(s, axis=-1, keepdims=True)
    e = jnp.exp(s - m)
    den = jnp.sum(e, axis=-1, keepdims=True)
    lse = (m + jnp.log(den))[..., 0]
    o = jnp.einsum('nbhqk,nbkhd->nbqhd', (e / den).astype(v.dtype), vb)
    o = o.reshape(n, lp, h, d)[:, :l]
    lse = lse.swapaxes(2, 3).reshape(n, lp, h)[:, :l]
    return o, lse


def _to_strided(t, dil):
    b, s = t.shape[:2]
    rest = t.shape[2:]
    t = t.reshape((b, s // dil, dil) + rest).swapaxes(1, 2)
    return t.reshape((b * dil, s // dil) + rest)


def _from_strided(t, b, dil):
    l = t.shape[1]
    rest = t.shape[2:]
    t = t.reshape((b, dil, l) + rest).swapaxes(1, 2)
    return t.reshape((b, l * dil) + rest)


def _diff_mixer(h_a, positions, lam_params, subln_g, layer_idx):
    b, s, _ = h_a.shape
    q, k, v = jnp.split(h_a, 3, axis=-1)
    q = q.reshape(b, s, DIFF_HEADS, 2, DIFF_QK_DIM)
    k = k.reshape(b, s, DIFF_HEADS, 2, DIFF_QK_DIM)
    v = v.reshape(b, s, DIFF_HEADS, HEAD_DIM)
    rot = DIFF_QK_DIM // ROPE_FRACTION
    q1 = _rope(q[:, :, :, 0], positions, rot)
    q2 = _rope(q[:, :, :, 1], positions, rot)
    k1 = _rope(k[:, :, :, 0], positions, rot)
    k2 = _rope(k[:, :, :, 1], positions, rot)
    lam_init = 0.8 - 0.6 * math.exp(-0.3 * layer_idx)
    lp = lam_params.astype(jnp.float32)
    lam = jnp.exp(jnp.sum(lp[0] * lp[1])) - jnp.exp(jnp.sum(lp[2] * lp[3])) + lam_init
    o = _diff_attention(q1, k1, q2, k2, v, lam, DIFF_QK_DIM ** -0.5)
    o = _rms_norm(o, subln_g) * (1.0 - lam_init)
    return o.reshape(b, s, GROUP_WIDTH)


def _dilated_mixer(h_b, positions):
    b, s, _ = h_b.shape
    q, k, v = jnp.split(h_b, 3, axis=-1)
    rot = HEAD_DIM // ROPE_FRACTION
    q = _rope(q.reshape(b, s, DIL_HEADS, HEAD_DIM), positions, rot)
    k = _rope(k.reshape(b, s, DIL_HEADS, HEAD_DIM), positions, rot)
    v = v.reshape(b, s, DIL_HEADS, HEAD_DIM)
    outs, lses = [], []
    for window, dil in DIL_PATTERNS:
        half = window // 2 // dil
        o, lse = _banded_attention(_to_strided(q, dil), _to_strided(k, dil), _to_strided(v, dil), half)
        outs.append(_from_strided(o, b, dil))
        lses.append(_from_strided(lse, b, dil))
    wts = jax.nn.softmax(jnp.stack(lses, axis=0), axis=0)
    o = jnp.sum(wts[..., None] * jnp.stack(outs, axis=0).astype(jnp.float32), axis=0)
    return o.astype(h_b.dtype).reshape(b, s, GROUP_WIDTH)


def _pool_mixer(h_c, pool_w, pool_scale):
    b, s, _ = h_c.shape
    u = h_c.astype(jnp.float32)
    cs = jnp.concatenate([jnp.zeros((b, 1, GROUP_WIDTH), jnp.float32), jnp.cumsum(u, axis=1)], axis=1)
    t = jnp.arange(s)
    diffs = []
    for g, w in enumerate(POOL_WINDOWS):
        lo = jnp.clip(t - w // 2, 0, s - 1)
        hi = jnp.clip(t + w - w // 2 - 1, 0, s - 1)
        csg = cs[..., g * POOL_GROUP:(g + 1) * POOL_GROUP]
        cnt = (hi - lo + 1).astype(jnp.float32)[None, :, None]
        mean = (csg[:, hi + 1] - csg[:, lo]) / cnt
        diffs.append(mean - u[..., g * POOL_GROUP:(g + 1) * POOL_GROUP])
    d = jnp.stack(diffs, axis=2).astype(h_c.dtype)
    y = jnp.einsum('bsgc,gcd->bsgd', d, pool_w).reshape(b, s, GROUP_WIDTH)
    return y * pool_scale


def _mla_mixer(h_d, positions, q_norm_g, kv_norm_g, w_uq, w_ukv):
    b, s, _ = h_d.shape
    c_q, c_kv, k_r = jnp.split(h_d, [MLA_Q_RANK, MLA_Q_RANK + MLA_KV_RANK], axis=-1)
    q = (_rms_norm(c_q, q_norm_g) @ w_uq).reshape(b, s, MLA_HEADS, MLA_NOPE_DIM + MLA_ROPE_DIM)
    q = jnp.concatenate([q[..., :MLA_NOPE_DIM], _rope(q[..., MLA_NOPE_DIM:], positions, MLA_ROPE_DIM)], axis=-1)
    kv = (_rms_norm(c_kv, kv_norm_g) @ w_ukv).reshape(b, s, MLA_HEADS, MLA_NOPE_DIM + MLA_V_DIM)
    k_nope, v = kv[..., :MLA_NOPE_DIM], kv[..., MLA_NOPE_DIM:]
    k_rope = _rope(k_r[:, :, None, :], positions, MLA_ROPE_DIM)
    k = jnp.concatenate([k_nope, jnp.broadcast_to(k_rope, (b, s, MLA_HEADS, MLA_ROPE_DIM))], axis=-1)
    o = _dense_attention(q, k, v, (MLA_NOPE_DIM + MLA_ROPE_DIM) ** -0.5)
    return o.reshape(b, s, GROUP_WIDTH)


def _conv_ffn(x, w_up, conv_w, conv_b, w_down):
    u = x @ w_up
    u = lax.conv_general_dilated(u, conv_w[:, None, :], window_strides=(1,),
                                 padding=((CONV_WIDTH // 2, CONV_WIDTH // 2),),
                                 dimension_numbers=('NWC', 'WIO', 'NWC'),
                                 feature_group_count=u.shape[-1]) + conv_b
    gate, up = jnp.split(u, 2, axis=-1)
    return (jax.nn.silu(gate) * up) @ w_down


def setup_inputs(seed: int = 0) -> dict:
    key = jax.random.key(seed)
    ks = jax.random.split(key, 20)
    f32 = jnp.float32
    nrm = lambda k, shape, scale: jax.random.normal(k, shape, f32) * scale
    x = jax.random.normal(ks[0], (BATCH, SEQ, D_MODEL), f32)
    offsets = jax.random.randint(ks[1], (BATCH, 1), 0, 4096, dtype=jnp.int32)
    positions = offsets + jnp.arange(SEQ, dtype=jnp.int32)[None, :]
    return {
        'x': x,
        'positions': positions,
        'w_in': nrm(ks[2], (DEPTH, D_MODEL, IN_COLS), D_MODEL ** -0.5),
        'diff_lambda': nrm(ks[3], (DEPTH, 4, DIFF_QK_DIM), 0.1),
        'diff_subln': 1.0 + nrm(ks[4], (DEPTH, HEAD_DIM), 0.02),
        'pool_w': nrm(ks[5], (DEPTH, len(POOL_WINDOWS), POOL_GROUP, POOL_GROUP), POOL_GROUP ** -0.5),
        'pool_scale': 1.0 + nrm(ks[6], (DEPTH, GROUP_WIDTH), 0.1),
        'mla_q_norm': 1.0 + nrm(ks[7], (DEPTH, MLA_Q_RANK), 0.02),
        'mla_kv_norm': 1.0 + nrm(ks[8], (DEPTH, MLA_KV_RANK), 0.02),
        'mla_w_uq': nrm(ks[9], (DEPTH, MLA_Q_RANK, MLA_HEADS * (MLA_NOPE_DIM + MLA_ROPE_DIM)), MLA_Q_RANK ** -0.5),
        'mla_w_ukv': nrm(ks[10], (DEPTH, MLA_KV_RANK, MLA_HEADS * (MLA_NOPE_DIM + MLA_V_DIM)), MLA_KV_RANK ** -0.5),
        'w_out': nrm(ks[11], (DEPTH, MIX_WIDTH, D_MODEL), DN_BETA * MIX_WIDTH ** -0.5),
        'ln1_g': 1.0 + nrm(ks[12], (DEPTH, D_MODEL), 0.02),
        'ln1_b': nrm(ks[13], (DEPTH, D_MODEL), 0.02),
        'ffn_w_up': nrm(ks[14], (DEPTH, D_MODEL, 2 * D_FF), D_MODEL ** -0.5),
        'ffn_conv_w': nrm(ks[15], (DEPTH, CONV_WIDTH, 2 * D_FF), CONV_WIDTH ** -0.5),
        'ffn_conv_b': nrm(ks[16], (DEPTH, 2 * D_FF), 0.02),
        'ffn_w_down': nrm(ks[17], (DEPTH, D_FF, D_MODEL), DN_BETA * D_FF ** -0.5),
        'ln2_g': 1.0 + nrm(ks[18], (DEPTH, D_MODEL), 0.02),
        'ln2_b': nrm(ks[19], (DEPTH, D_MODEL), 0.02),
    }


def reference(x, positions, w_in, diff_lambda, diff_subln, pool_w, pool_scale, mla_q_norm, mla_kv_norm,
              mla_w_uq, mla_w_ukv, w_out, ln1_g, ln1_b, ffn_w_up, ffn_conv_w, ffn_conv_b, ffn_w_down,
              ln2_g, ln2_b):
    for l in range(DEPTH):
        h = x @ w_in[l]
        h_a, h_b, h_c, h_d = jnp.split(h, [A_COLS, A_COLS + B_COLS, A_COLS + B_COLS + C_COLS], axis=-1)
        y_a = _diff_mixer(h_a, positions, diff_lambda[l], diff_subln[l], l)
        y_b = _dilated_mixer(h_b, positions)
        y_c = _pool_mixer(h_c, pool_w[l], pool_scale[l])
        y_d = _mla_mixer(h_d, positions, mla_q_norm[l], mla_kv_norm[l], mla_w_uq[l], mla_w_ukv[l])
        mix = jnp.concatenate([y_a, y_b, y_c, y_d], axis=-1)
        x = _layer_norm(DN_ALPHA * x + mix @ w_out[l], ln1_g[l], ln1_b[l])
        f = _conv_ffn(x, ffn_w_up[l], ffn_conv_w[l], ffn_conv_b[l], ffn_w_down[l])
        x = _layer_norm(DN_ALPHA * x + f, ln2_g[l], ln2_b[l])
    return x
```

```python
import functools
import math

import jax
import jax.numpy as jnp
from jax import lax
from jax.experimental import pallas as pl
from jax.experimental.pallas import tpu as pltpu

F32 = jnp.float32
BF16 = jnp.bfloat16

D_MODEL = 1024
DEPTH = 4
HEAD_DIM = 64
GROUP_WIDTH = 256
ROPE_THETA = 500000.0
NEG_INF = -1e30

DIFF_QK_DIM = 32
DIL_PATTERNS = ((128, 1), (512, 4), (2048, 16))
POOL_WINDOWS = (2, 4, 8, 16)
POOL_GROUP = 64
MLA_Q_RANK = 256
MLA_KV_RANK = 128
MLA_NOPE_DIM = 64
MLA_ROPE_DIM = 32
D_FF = 2816
DN_ALPHA = (2 * DEPTH) ** 0.25
LN_EPS = 1e-5
RMS_EPS = 1e-6

LANES = 128
SUM_ROWS = 16
VT_ROWS = HEAD_DIM + SUM_ROWS
DIL_HALO = 1024
VMEM_LIMIT = 56 * 1024 * 1024

PROJ_TM = 512
ATT_TK = 256
DIFF_TQ = 256
MLA_TQ = 512
DIL_TQ = 256
POOL_TM = 512
OUT_TM = 512
FFN_TM = 512
FFN_CHUNK = 256
HALO = 8


def _cparams(*sem):
    return pltpu.CompilerParams(dimension_semantics=sem, vmem_limit_bytes=VMEM_LIMIT)


def _rope_tables_kernel(pos_ref, c_ref, o_ref):
    pos = pos_ref[...].astype(F32)
    ang = pos * c_ref[0:1, :]
    cs = jnp.cos(ang)
    sn = jnp.sin(ang)
    is_rope = c_ref[1:2, :]
    o_ref[0] = cs * is_rope + (1.0 - is_rope)
    o_ref[1] = -sn * c_ref[2:3, :]
    o_ref[2] = sn * c_ref[3:4, :]


def _rope_consts(period, rot_dim, lane0):
    half = rot_dim // 2
    inv_freq = ROPE_THETA ** (-jnp.arange(half, dtype=F32) * 2.0 / rot_dim)
    lane = jnp.arange(LANES)
    d = lane % period - lane0
    is_rope = (d >= 0) & (d < rot_dim)
    f_idx = jnp.clip(d, 0, rot_dim - 1) % half
    rows = [
        jnp.where(is_rope, inv_freq[f_idx], 0.0),
        is_rope.astype(F32),
        (is_rope & (d < half)).astype(F32),
        (is_rope & (d >= half)).astype(F32),
    ]
    rows += [jnp.zeros((LANES,), F32)] * 4
    return jnp.stack(rows).astype(F32)


def _rope_tables(pos2d, consts):
    t = pos2d.shape[0]
    tm = 1024
    n = consts.shape[0]
    return pl.pallas_call(
        _rope_tables_kernel,
        grid=(n, t // tm),
        in_specs=[
            pl.BlockSpec((tm, 1), lambda j, i: (i, 0)),
            pl.BlockSpec((None, 8, LANES), lambda j, i: (j, 0, 0)),
        ],
        out_specs=pl.BlockSpec((None, 3, tm, LANES), lambda j, i: (j, 0, i, 0)),
        out_shape=jax.ShapeDtypeStruct((n, 3, t, LANES), F32),
        compiler_params=_cparams("arbitrary", "arbitrary"),
        name="rope_tables",
    )(pos2d, consts)


def _rope_apply(h, t_ref, half):
    cos, sa, sb = t_ref[0], t_ref[1], t_ref[2]
    outs = []
    for j in range(h.shape[1] // LANES):
        xj = h[:, LANES * j:LANES * (j + 1)]
        fwd = pltpu.roll(xj, LANES - half, 1)
        bwd = pltpu.roll(xj, half, 1)
        outs.append(xj * cos + fwd * sa + bwd * sb)
    return outs[0] if len(outs) == 1 else jnp.concatenate(outs, axis=1)


PROJ_COLS = 6 * GROUP_WIDTH + GROUP_WIDTH + 512


def _proj_kernel(x_ref, w_ref, ta_ref, tb_ref,
                 qa_ref, ka_ref, va_ref, qb_ref, kb_ref, vb_ref, hc_ref, hd_ref):
    x = x_ref[...].astype(BF16)

    def seg(j0, n):
        return jnp.dot(x, w_ref[:, j0:j0 + n], preferred_element_type=F32)

    g = GROUP_WIDTH
    qa_ref[...] = (_rope_apply(seg(0, g), ta_ref, 4) * (DIFF_QK_DIM ** -0.5)).astype(BF16)
    ka_ref[...] = _rope_apply(seg(g, g), ta_ref, 4).astype(BF16)
    va_ref[...] = seg(2 * g, g).astype(BF16)
    qb_ref[...] = (_rope_apply(seg(3 * g, g), tb_ref, 8) * (HEAD_DIM ** -0.5)).astype(BF16)
    kb_ref[...] = _rope_apply(seg(4 * g, g), tb_ref, 8).astype(BF16)
    vb_ref[...] = seg(5 * g, g).astype(BF16)
    hc_ref[...] = seg(6 * g, g)
    hd_ref[...] = seg(7 * g, 512)


def _proj(x2d, w, ta, tb):
    t = x2d.shape[0]
    tm = PROJ_TM
    g = GROUP_WIDTH
    row = lambda n: pl.BlockSpec((tm, n), lambda i: (i, 0))
    tab = pl.BlockSpec((3, tm, LANES), lambda i: (0, i, 0))
    bshape = lambda n, dt: jax.ShapeDtypeStruct((t, n), dt)
    return pl.pallas_call(
        _proj_kernel,
        grid=(t // tm,),
        in_specs=[row(D_MODEL), pl.BlockSpec((D_MODEL, PROJ_COLS), lambda i: (0, 0)), tab, tab],
        out_specs=[row(g)] * 7 + [row(512)],
        out_shape=[bshape(g, BF16)] * 6 + [bshape(g, F32), bshape(512, F32)],
        compiler_params=_cparams("arbitrary"),
        name="proj_rope",
    )(x2d, w, ta, tb)


def _rms(x, g):
    return x * lax.rsqrt(jnp.mean(x * x, axis=-1, keepdims=True) + RMS_EPS) * g


def _mla_prep_kernel(hd_ref, td_ref, gq_ref, gkv_ref, wq_ref, wk_ref, wv_ref,
                     q_ref, k_ref, v_ref):
    hd = hd_ref[...]
    cq = _rms(hd[:, 0:MLA_Q_RANK], gq_ref[...]).astype(BF16)
    ckv = _rms(hd[:, MLA_Q_RANK:MLA_Q_RANK + MLA_KV_RANK], gkv_ref[...]).astype(BF16)
    q = jnp.dot(cq, wq_ref[...], preferred_element_type=F32)
    scale = (MLA_NOPE_DIM + MLA_ROPE_DIM) ** -0.5
    q_ref[...] = (_rope_apply(q, td_ref, MLA_ROPE_DIM // 2) * scale).astype(BF16)
    k_rope = _rope_apply(hd[:, 384:512], td_ref, MLA_ROPE_DIM // 2)
    k = jnp.dot(ckv, wk_ref[...], preferred_element_type=F32)
    k_ref[...] = (k + jnp.concatenate([k_rope] * 4, axis=1)).astype(BF16)
    v_ref[...] = jnp.dot(ckv, wv_ref[...], preferred_element_type=F32).astype(BF16)


def _mla_prep(hd, td, gq, gkv, wq, wk, wv):
    t = hd.shape[0]
    tm = PROJ_TM
    row = lambda n: pl.BlockSpec((tm, n), lambda i: (i, 0))
    full = lambda a: pl.BlockSpec(a.shape, lambda i: (0,) * a.ndim)
    return pl.pallas_call(
        _mla_prep_kernel,
        grid=(t // tm,),
        in_specs=[row(512), pl.BlockSpec((3, tm, LANES), lambda i: (0, i, 0)),
                  full(gq), full(gkv), full(wq), full(wk), full(wv)],
        out_specs=[row(512), row(512), row(GROUP_WIDTH)],
        out_shape=[jax.ShapeDtypeStruct((t, 512), BF16), jax.ShapeDtypeStruct((t, 512), BF16),
                   jax.ShapeDtypeStruct((t, GROUP_WIDTH), BF16)],
        compiler_params=_cparams("arbitrary"),
        name="mla_prep",
    )(hd, td, gq, gkv, wq, wk, wv)


def _flash_sweep(n_chunks, k_chunk, w, v_chunks, widths, mask_chunk=None):
    n = w.shape[1]
    offs = [sum(widths[:g]) for g in range(len(widths))]

    def body(c, carry):
        m, accs = carry
        s = jnp.dot(k_chunk(c), w, preferred_element_type=F32)
        if mask_chunk is not None:
            mult = mask_chunk(c)
            s = jnp.where(mult > 0.0, s, NEG_INF)
        m_new = jnp.maximum(m, jnp.max(s, axis=0, keepdims=True))
        alpha = jnp.exp(m - m_new)
        p = jnp.exp(s - m_new)
        if mask_chunk is not None:
            p = p * mult
        p = p.astype(BF16)
        new = []
        for g, acc in enumerate(accs):
            lo, hi = offs[g], offs[g] + widths[g]
            pv = jnp.dot(v_chunks[g](c), p[:, lo:hi], preferred_element_type=F32)
            new.append(acc * alpha[:, lo:hi] + pv)
        return m_new, tuple(new)

    init = (jnp.full((1, n), NEG_INF, F32),
            tuple(jnp.zeros((VT_ROWS, wd), F32) for wd in widths))
    _, accs = lax.fori_loop(0, n_chunks, body, init)
    return accs


def _diff_kernel(lam_ref, g_ref, qt_ref, k_ref, vt_ref, o_ref, *, lam_init, n_chunks):
    tq = qt_ref.shape[1]
    tk = vt_ref.shape[3]
    lp = lam_ref[...]
    lam = (jnp.exp(jnp.sum(lp[0:1] * lp[1:2], axis=1, keepdims=True))
           - jnp.exp(jnp.sum(lp[2:3] * lp[3:4], axis=1, keepdims=True)) + lam_init)

    qt = qt_ref[...].astype(F32)
    row = lax.broadcasted_iota(jnp.int32, (LANES, tq), 0)
    cols = [jnp.where((row >= DIFF_QK_DIM * c) & (row < DIFF_QK_DIM * (c + 1)), qt, 0.0)
            for c in range(4)]
    w = jnp.concatenate(cols, axis=1).astype(BF16)

    accs = _flash_sweep(
        n_chunks,
        lambda c: k_ref[pl.ds(pl.multiple_of(c * tk, tk), tk), :],
        w,
        [lambda c: vt_ref[0, c], lambda c: vt_ref[1, c]],
        [2 * tq, 2 * tq])

    gain = g_ref[...] * (1.0 - lam_init)
    for hl, acc in enumerate(accs):
        o1 = acc[0:HEAD_DIM, 0:tq] / acc[HEAD_DIM:HEAD_DIM + 1, 0:tq]
        o2 = acc[0:HEAD_DIM, tq:2 * tq] / acc[HEAD_DIM:HEAD_DIM + 1, tq:2 * tq]
        o = o1 - lam * o2
        ms = jnp.mean(o * o, axis=0, keepdims=True)
        y = o * lax.rsqrt(ms + RMS_EPS) * gain
        o_ref[HEAD_DIM * hl:HEAD_DIM * (hl + 1), :] = y.astype(BF16)


def _diff_attention(lam_params, subln_g, qt, k, vt, layer_idx):
    b, _, s = qt.shape
    tq, tk = DIFF_TQ, ATT_TK
    nck = s // tk
    lam_init = 0.8 - 0.6 * math.exp(-0.3 * layer_idx)
    kern = functools.partial(_diff_kernel, lam_init=lam_init, n_chunks=nck)
    return pl.pallas_call(
        kern,
        grid=(b, 2, s // tq),
        in_specs=[
            pl.BlockSpec((4, DIFF_QK_DIM), lambda bi, j, qi: (0, 0)),
            pl.BlockSpec((HEAD_DIM, 1), lambda bi, j, qi: (0, 0)),
            pl.BlockSpec((None, LANES, tq), lambda bi, j, qi: (bi, j, qi)),
            pl.BlockSpec((None, s, LANES), lambda bi, j, qi: (bi, 0, j)),
            pl.BlockSpec((None, 2, nck, VT_ROWS, tk), lambda bi, j, qi: (bi, j, 0, 0, 0)),
        ],
        out_specs=pl.BlockSpec((None, LANES, tq), lambda bi, j, qi: (bi, j, qi)),
        out_shape=jax.ShapeDtypeStruct((b, GROUP_WIDTH, s), BF16),
        compiler_params=_cparams("arbitrary", "arbitrary", "arbitrary"),
        name="diff_attention",
    )(lam_params, subln_g, qt, k, vt)


def _mla_kernel(qt_ref, k_ref, vt_ref, o_ref, *, n_chunks):
    tq = qt_ref.shape[1]
    tk = vt_ref.shape[2]
    (acc,) = _flash_sweep(
        n_chunks,
        lambda c: k_ref[pl.ds(pl.multiple_of(c * tk, tk), tk), :],
        qt_ref[...],
        [lambda c: vt_ref[c]],
        [tq])
    o_ref[...] = (acc[0:HEAD_DIM] / acc[HEAD_DIM:HEAD_DIM + 1]).astype(BF16)


def _mla_attention(qt, k, vt):
    b, _, s = qt.shape
    tq, tk = MLA_TQ, ATT_TK
    nck = s // tk
    return pl.pallas_call(
        functools.partial(_mla_kernel, n_chunks=nck),
        grid=(b, 4, s // tq),
        in_specs=[
            pl.BlockSpec((None, LANES, tq), lambda bi, h, qi: (bi, h, qi)),
            pl.BlockSpec((None, s, LANES), lambda bi, h, qi: (bi, 0, h)),
            pl.BlockSpec((None, None, nck, VT_ROWS, tk), lambda bi, h, qi: (bi, h, 0, 0, 0)),
        ],
        out_specs=pl.BlockSpec((None, HEAD_DIM, tq), lambda bi, h, qi: (bi, h, qi)),
        out_shape=jax.ShapeDtypeStruct((b, GROUP_WIDTH, s), BF16),
        compiler_params=_cparams("arbitrary", "arbitrary", "arbitrary"),
        name="mla_attention",
    )(qt, k, vt)


def _dil_multiplicity(tq, tk):
    n_win = (tq + 2 * DIL_HALO) // tk
    key = jnp.arange(n_win * tk)[:, None] - DIL_HALO
    delta = key - jnp.arange(tq)[None, :]
    mult = jnp.zeros(delta.shape, F32)
    for window, dil in DIL_PATTERNS:
        reach = (window // 2 // dil) * dil
        mult = mult + ((delta % dil == 0) & (jnp.abs(delta) <= reach)).astype(F32)
    return mult.reshape(n_win, tk, tq)


def _dil_kernel(c_ref, qt_ref, k_ref, vt_ref, o_ref, *, n_seq_chunks):
    tq = qt_ref.shape[1]
    n_win, tk = c_ref.shape[0], c_ref.shape[1]
    halo_chunks = DIL_HALO // tk
    qi = pl.program_id(2)

    qt = qt_ref[...].astype(F32)
    row = lax.broadcasted_iota(jnp.int32, (LANES, tq), 0)
    w = jnp.concatenate([jnp.where(row < HEAD_DIM, qt, 0.0),
                         jnp.where(row >= HEAD_DIM, qt, 0.0)], axis=1).astype(BF16)

    def mask_chunk(c):
        ck = qi * (tq // tk) + c
        inside = jnp.where((ck >= halo_chunks) & (ck < halo_chunks + n_seq_chunks), 1.0, 0.0)
        mult = c_ref[c] * inside
        return jnp.concatenate([mult, mult], axis=1)

    def k_chunk(c):
        ck = qi * (tq // tk) + c
        return k_ref[pl.ds(pl.multiple_of(ck * tk, tk), tk), :]

    accs = _flash_sweep(
        n_win, k_chunk, w,
        [lambda c: vt_ref[0, qi * (tq // tk) + c], lambda c: vt_ref[1, qi * (tq // tk) + c]],
        [tq, tq], mask_chunk=mask_chunk)
    for hl, acc in enumerate(accs):
        o_ref[HEAD_DIM * hl:HEAD_DIM * (hl + 1), :] = (
            acc[0:HEAD_DIM] / acc[HEAD_DIM:HEAD_DIM + 1]).astype(BF16)


def _dil_attention(mult, qt, k_pad, vt_pad):
    b, _, s = qt.shape
    tq, tk = DIL_TQ, ATT_TK
    n_win = mult.shape[0]
    sp = s + 2 * DIL_HALO
    return pl.pallas_call(
        functools.partial(_dil_kernel, n_seq_chunks=s // tk),
        grid=(b, 2, s // tq),
        in_specs=[
            pl.BlockSpec((n_win, tk, tq), lambda bi, j, qi: (0, 0, 0)),
            pl.BlockSpec((None, LANES, tq), lambda bi, j, qi: (bi, j, qi)),
            pl.BlockSpec((None, sp, LANES), lambda bi, j, qi: (bi, 0, j)),
            pl.BlockSpec((None, 2, sp // tk, VT_ROWS, tk), lambda bi, j, qi: (bi, j, 0, 0, 0)),
        ],
        out_specs=pl.BlockSpec((None, LANES, tq), lambda bi, j, qi: (bi, j, qi)),
        out_shape=jax.ShapeDtypeStruct((b, GROUP_WIDTH, s), BF16),
        compiler_params=_cparams("arbitrary", "arbitrary", "arbitrary"),
        name="dilated_attention",
    )(mult, qt, k_pad, vt_pad)


def _with_halo(prev_ref, cur_ref, next_ref, i, n_tiles):
    prev = jnp.where(i > 0, prev_ref[...], 0.0)
    nxt = jnp.where(i < n_tiles - 1, next_ref[...], 0.0)
    return jnp.concatenate([prev, cur_ref[...], nxt], axis=0)


def _shift_rows(x, k):
    n = x.shape[0]
    return pltpu.roll(x, (n - k) % n, 0)


def _pool_kernel(prev_ref, cur_ref, next_ref, w_ref, scale_ref, o_ref, *, seq_len):
    tm = cur_ref.shape[0]
    i = pl.program_id(1)
    x = _with_halo(prev_ref, cur_ref, next_ref, i, seq_len // tm)
    t = i * tm + lax.broadcasted_iota(jnp.int32, (tm, 1), 0)
    lane_group = lax.broadcasted_iota(jnp.int32, (tm, GROUP_WIDTH), 1) // POOL_GROUP

    run = x
    mean = None
    for g, wnd in enumerate(POOL_WINDOWS):
        run = run + _shift_rows(run, wnd // 2)
        total = _shift_rows(run, -(wnd // 2))[HALO:HALO + tm]
        lo = jnp.clip(t - wnd // 2, 0, seq_len - 1)
        hi = jnp.clip(t + wnd - wnd // 2 - 1, 0, seq_len - 1)
        cand = total / (hi - lo + 1).astype(F32)
        mean = cand if mean is None else jnp.where(lane_group == g, cand, mean)
    d = (mean - cur_ref[...]).astype(BF16)
    y = jnp.dot(d, w_ref[...], preferred_element_type=F32) * scale_ref[...]
    o_ref[...] = y.astype(BF16)


def _halo_specs(tm, cols):
    nb = tm // HALO
    return [
        pl.BlockSpec((None, HALO, cols), lambda bi, i: (bi, jnp.maximum(i * nb - 1, 0), 0)),
        pl.BlockSpec((None, tm, cols), lambda bi, i: (bi, i, 0)),
        pl.BlockSpec((None, HALO, cols), lambda bi, i, nb=nb: (bi, (i + 1) * nb, 0)),
    ]


def _pool_mixer(hc3, w_bd, scale):
    b, s, c = hc3.shape
    tm = POOL_TM
    prev_s, cur_s, next_s = _halo_specs(tm, c)
    last = s // HALO - 1
    next_s = pl.BlockSpec((None, HALO, c),
                          lambda bi, i: (bi, jnp.minimum((i + 1) * (tm // HALO), last), 0))
    return pl.pallas_call(
        functools.partial(_pool_kernel, seq_len=s),
        grid=(b, s // tm),
        in_specs=[prev_s, cur_s, next_s,
                  pl.BlockSpec(w_bd.shape, lambda bi, i: (0, 0)),
                  pl.BlockSpec(scale.shape, lambda bi, i: (0, 0))],
        out_specs=pl.BlockSpec((None, tm, c), lambda bi, i: (bi, i, 0)),
        out_shape=jax.ShapeDtypeStruct((b, s, c), BF16),
        compiler_params=_cparams("arbitrary", "arbitrary"),
        name="pool_mixer",
    )(hc3, hc3, hc3, w_bd, scale)


def _layer_norm(z, g, b):
    mu = jnp.mean(z, axis=-1, keepdims=True)
    zc = z - mu
    var = jnp.mean(zc * zc, axis=-1, keepdims=True)
    return zc * lax.rsqrt(var + LN_EPS) * g + b


def _tdot(at, w):
    return lax.dot_general(at, w, (((0,), (0,)), ((), ())), preferred_element_type=F32)


def _out_kernel(x_ref, ya_ref, yb_ref, yc_ref, yd_ref, w_ref, g_ref, b_ref, o_ref):
    gw = GROUP_WIDTH
    z = _tdot(ya_ref[...], w_ref[0:gw, :])
    z = z + _tdot(yb_ref[...], w_ref[gw:2 * gw, :])
    z = z + jnp.dot(yc_ref[...], w_ref[2 * gw:3 * gw, :], preferred_element_type=F32)
    z = z + _tdot(yd_ref[...], w_ref[3 * gw:4 * gw, :])
    o_ref[...] = _layer_norm(DN_ALPHA * x_ref[...] + z, g_ref[...], b_ref[...])


def _out_proj(x3, ya_t, yb_t, yc, yd_t, w, g, bias):
    b, s, d = x3.shape
    tm = OUT_TM
    tok = lambda n: pl.BlockSpec((None, tm, n), lambda bi, i: (bi, i, 0))
    feat = pl.BlockSpec((None, GROUP_WIDTH, tm), lambda bi, i: (bi, 0, i))
    full = lambda a: pl.BlockSpec(a.shape, lambda bi, i: (0,) * a.ndim)
    return pl.pallas_call(
        _out_kernel,
        grid=(b, s // tm),
        in_specs=[tok(d), feat, feat, tok(GROUP_WIDTH), feat, full(w), full(g), full(bias)],
        out_specs=tok(d),
        out_shape=jax.ShapeDtypeStruct((b, s, d), F32),
        compiler_params=_cparams("arbitrary", "arbitrary"),
        name="out_proj_ln",
    )(x3, ya_t, yb_t, yc, yd_t, w, g, bias)


def _ffn_kernel(prev_ref, cur_ref, next_ref, wg_ref, wu_ref, wd_ref, cg_ref, cu_ref,
                g_ref, b_ref, o_ref, acc_ref, *, seq_len):
    tm = cur_ref.shape[0]
    i = pl.program_id(1)
    xe = _with_halo(prev_ref, cur_ref, next_ref, i, seq_len // tm).astype(BF16)

    def conv(u, c):
        y = _shift_rows(u, -1) * c[0:1] + u * c[1:2] + _shift_rows(u, 1) * c[2:3] + c[3:4]
        return y[HALO:HALO + tm]

    acc_ref[...] = jnp.zeros_like(acc_ref)

    def body(c, carry):
        gate = conv(jnp.dot(xe, wg_ref[c], preferred_element_type=F32), cg_ref[c])
        up = conv(jnp.dot(xe, wu_ref[c], preferred_element_type=F32), cu_ref[c])
        act = (gate * jax.nn.sigmoid(gate) * up).astype(BF16)
        acc_ref[...] += jnp.dot(act, wd_ref[c], preferred_element_type=F32)
        return carry

    lax.fori_loop(0, wg_ref.shape[0], body, 0)
    o_ref[...] = _layer_norm(DN_ALPHA * cur_ref[...] + acc_ref[...], g_ref[...], b_ref[...])


def _ffn(x3, wg, wu, wd, cg, cu, g, bias):
    b, s, d = x3.shape
    tm = FFN_TM
    prev_s, cur_s, _ = _halo_specs(tm, d)
    last = s // HALO - 1
    next_s = pl.BlockSpec((None, HALO, d),
                          lambda bi, i: (bi, jnp.minimum((i + 1) * (tm // HALO), last), 0))
    full = lambda a: pl.BlockSpec(a.shape, lambda bi, i: (0,) * a.ndim)
    return pl.pallas_call(
        functools.partial(_ffn_kernel, seq_len=s),
        grid=(b, s // tm),
        in_specs=[prev_s, cur_s, next_s, full(wg), full(wu), full(wd), full(cg), full(cu),
                  full(g), full(bias)],
        out_specs=pl.BlockSpec((None, tm, d), lambda bi, i: (bi, i, 0)),
        out_shape=jax.ShapeDtypeStruct((b, s, d), F32),
        scratch_shapes=[pltpu.VMEM((tm, d), F32)],
        compiler_params=_cparams("arbitrary", "arbitrary"),
        name="conv_ffn_ln",
    )(x3, x3, x3, wg, wu, wd, cg, cu, g, bias)


def _feature_major(a, b, s):
    return a.reshape(b, s, a.shape[-1]).transpose(0, 2, 1)


def _values_t(v, b, s, tk, pad=0):
    vt = v.reshape(b, s, 4, HEAD_DIM).transpose(0, 2, 3, 1)
    vt = jnp.concatenate([vt, jnp.ones((b, 4, SUM_ROWS, s), v.dtype)], axis=2)
    if pad:
        vt = jnp.pad(vt, ((0, 0), (0, 0), (0, 0), (pad, pad)))
    sp = s + 2 * pad
    return vt.reshape(b, 4, VT_ROWS, sp // tk, tk).transpose(0, 1, 3, 2, 4)


def _pad_cols(w, groups, width, padded):
    k = w.shape[0]
    w = w.reshape(k, groups, width)
    return jnp.pad(w, ((0, 0), (0, 0), (0, padded - width))).reshape(k, groups * padded)


def kernel(x, positions, w_in, diff_lambda, diff_subln, pool_w, pool_scale, mla_q_norm,
           mla_kv_norm, mla_w_uq, mla_w_ukv, w_out, ln1_g, ln1_b, ffn_w_up, ffn_conv_w,
           ffn_conv_b, ffn_w_down, ln2_g, ln2_b):
    b, s, d = x.shape
    t = b * s
    assert d == D_MODEL and s % 1024 == 0

    consts = jnp.stack([
        _rope_consts(DIFF_QK_DIM, DIFF_QK_DIM // 4, 0),
        _rope_consts(HEAD_DIM, HEAD_DIM // 4, 0),
        _rope_consts(LANES, MLA_ROPE_DIM, MLA_NOPE_DIM),
    ])
    tables = _rope_tables(positions.reshape(t, 1), consts)
    ta, tb, td = tables[0], tables[1], tables[2]
    mult = _dil_multiplicity(DIL_TQ, ATT_TK)

    n_ch = D_FF // FFN_CHUNK
    for l in range(DEPTH):
        wi = w_in[l]
        w_proj = jnp.concatenate([
            wi[:, :7 * GROUP_WIDTH + MLA_Q_RANK + MLA_KV_RANK],
            jnp.zeros((d, MLA_NOPE_DIM), wi.dtype),
            wi[:, 7 * GROUP_WIDTH + MLA_Q_RANK + MLA_KV_RANK:],
            jnp.zeros((d, LANES - MLA_NOPE_DIM - MLA_ROPE_DIM), wi.dtype)], axis=1).astype(BF16)
        wq = _pad_cols(mla_w_uq[l], 4, MLA_NOPE_DIM + MLA_ROPE_DIM, LANES).astype(BF16)
        wkv = mla_w_ukv[l].reshape(MLA_KV_RANK, 4, 2, HEAD_DIM)
        wk = _pad_cols(wkv[:, :, 0].reshape(MLA_KV_RANK, 4 * HEAD_DIM), 4, HEAD_DIM, LANES).astype(BF16)
        wv = wkv[:, :, 1].reshape(MLA_KV_RANK, 4 * HEAD_DIM).astype(BF16)
        pw = pool_w[l]
        w_pool = jnp.zeros((GROUP_WIDTH, GROUP_WIDTH), pw.dtype)
        for g in range(4):
            w_pool = w_pool.at[g * POOL_GROUP:(g + 1) * POOL_GROUP,
                               g * POOL_GROUP:(g + 1) * POOL_GROUP].set(pw[g])
        w_pool = w_pool.astype(BF16)
        wup = ffn_w_up[l]
        wg_c = wup[:, :D_FF].reshape(d, n_ch, FFN_CHUNK).transpose(1, 0, 2).astype(BF16)
        wu_c = wup[:, D_FF:].reshape(d, n_ch, FFN_CHUNK).transpose(1, 0, 2).astype(BF16)
        wd_c = ffn_w_down[l].reshape(n_ch, FFN_CHUNK, d).astype(BF16)
        conv = jnp.concatenate([ffn_conv_w[l], ffn_conv_b[l][None, :],
                                jnp.zeros((4, 2 * D_FF), F32)], axis=0)
        cg_c = conv[:, :D_FF].reshape(8, n_ch, FFN_CHUNK).transpose(1, 0, 2)
        cu_c = conv[:, D_FF:].reshape(8, n_ch, FFN_CHUNK).transpose(1, 0, 2)

        x2d = x.reshape(t, d)
        qa, ka, va, qb, kb, vb, hc, hd = _proj(x2d, w_proj, ta, tb)
        qd, kd, vd = _mla_prep(hd, td, mla_q_norm[l][None, :], mla_kv_norm[l][None, :], wq, wk, wv)

        ya_t = _diff_attention(diff_lambda[l], diff_subln[l][:, None],
                               _feature_major(qa, b, s), ka.reshape(b, s, GROUP_WIDTH),
                               _values_t(va, b, s, ATT_TK), l)
        kb_pad = jnp.pad(kb.reshape(b, s, GROUP_WIDTH), ((0, 0), (DIL_HALO, DIL_HALO), (0, 0)))
        yb_t = _dil_attention(mult, _feature_major(qb, b, s), kb_pad,
                              _values_t(vb, b, s, ATT_TK, pad=DIL_HALO))
        yc = _pool_mixer(hc.reshape(b, s, GROUP_WIDTH), w_pool, pool_scale[l][None, :])
        yd_t = _mla_attention(_feature_major(qd, b, s), kd.reshape(b, s, 512),
                              _values_t(vd, b, s, ATT_TK))

        x = _out_proj(x, ya_t, yb_t, yc, yd_t, w_out[l].astype(BF16),
                      ln1_g[l][None, :], ln1_b[l][None, :])
        x = _ffn(x, wg_c, wu_c, wd_c, cg_c, cu_c, ln2_g[l][None, :], ln2_b[l][None, :])
    return x
```

```python
import functools
import math

import jax
import jax.numpy as jnp
from jax import lax
from jax.experimental import pallas as pl
from jax.experimental.pallas import tpu as pltpu

F32 = jnp.float32
BF16 = jnp.bfloat16

D_MODEL = 1024
DEPTH = 4
HEAD_DIM = 64
GROUP_WIDTH = 256
ROPE_THETA = 500000.0
NEG_INF = -1e30
LOG2E = math.log2(math.e)

DIFF_QK_DIM = 32
DIL_PATTERNS = ((128, 1), (512, 4), (2048, 16))
POOL_WINDOWS = (2, 4, 8, 16)
POOL_GROUP = 64
MLA_Q_RANK = 256
MLA_KV_RANK = 128
MLA_NOPE_DIM = 64
MLA_ROPE_DIM = 32
D_FF = 2816
DN_ALPHA = (2 * DEPTH) ** 0.25
LN_EPS = 1e-5
RMS_EPS = 1e-6

LANES = 128
SUM_ROWS = 16
VT_ROWS = HEAD_DIM + SUM_ROWS
DIL_HALO = 1024
VMEM_LIMIT = 56 * 1024 * 1024

PROJ_TM = 512
ATT_TK = 256
DIFF_TQ = 256
MLA_TQ = 512
DIL_TQ = 256
POOL_TM = 512
OUT_TM = 512
FFN_TM = 512
FFN_CHUNK = 256
HALO = 8


def _cparams(*sem):
    return pltpu.CompilerParams(dimension_semantics=sem, vmem_limit_bytes=VMEM_LIMIT)


def _rope_tables_kernel(pos_ref, c_ref, o_ref):
    pos = pos_ref[...].astype(F32)
    ang = pos * c_ref[0:1, :]
    cs = jnp.cos(ang)
    sn = jnp.sin(ang)
    is_rope = c_ref[1:2, :]
    o_ref[0] = cs * is_rope + (1.0 - is_rope)
    o_ref[1] = -sn * c_ref[2:3, :]
    o_ref[2] = sn * c_ref[3:4, :]


def _rope_consts(period, rot_dim, lane0):
    half = rot_dim // 2
    inv_freq = ROPE_THETA ** (-jnp.arange(half, dtype=F32) * 2.0 / rot_dim)
    lane = jnp.arange(LANES)
    d = lane % period - lane0
    is_rope = (d >= 0) & (d < rot_dim)
    f_idx = jnp.clip(d, 0, rot_dim - 1) % half
    rows = [
        jnp.where(is_rope, inv_freq[f_idx], 0.0),
        is_rope.astype(F32),
        (is_rope & (d < half)).astype(F32),
        (is_rope & (d >= half)).astype(F32),
    ]
    rows += [jnp.zeros((LANES,), F32)] * 4
    return jnp.stack(rows).astype(F32)


def _rope_tables(pos2d, consts):
    t = pos2d.shape[0]
    tm = 1024
    n = consts.shape[0]
    return pl.pallas_call(
        _rope_tables_kernel,
        grid=(n, t // tm),
        in_specs=[
            pl.BlockSpec((tm, 1), lambda j, i: (i, 0)),
            pl.BlockSpec((None, 8, LANES), lambda j, i: (j, 0, 0)),
        ],
        out_specs=pl.BlockSpec((None, 3, tm, LANES), lambda j, i: (j, 0, i, 0)),
        out_shape=jax.ShapeDtypeStruct((n, 3, t, LANES), F32),
        compiler_params=_cparams("arbitrary", "arbitrary"),
        name="rope_tables",
    )(pos2d, consts)


def _rope_apply(h, t_ref, half):
    cos, sa, sb = t_ref[0], t_ref[1], t_ref[2]
    outs = []
    for j in range(h.shape[1] // LANES):
        xj = h[:, LANES * j:LANES * (j + 1)]
        fwd = pltpu.roll(xj, LANES - half, 1)
        bwd = pltpu.roll(xj, half, 1)
        outs.append(xj * cos + fwd * sa + bwd * sb)
    return outs[0] if len(outs) == 1 else jnp.concatenate(outs, axis=1)


PROJ_COLS = 6 * GROUP_WIDTH + GROUP_WIDTH + 512


def _proj_kernel(x_ref, w_ref, ta_ref, tb_ref,
                 qa_ref, ka_ref, va_ref, qb_ref, kb_ref, vb_ref, hc_ref, hd_ref):
    x = x_ref[...].astype(BF16)

    def seg(j0, n):
        return jnp.dot(x, w_ref[:, j0:j0 + n], preferred_element_type=F32)

    g = GROUP_WIDTH
    qa_ref[...] = (_rope_apply(seg(0, g), ta_ref, 4) * (LOG2E * DIFF_QK_DIM ** -0.5)).astype(BF16)
    ka_ref[...] = _rope_apply(seg(g, g), ta_ref, 4).astype(BF16)
    va_ref[...] = seg(2 * g, g).astype(BF16)
    qb_ref[...] = (_rope_apply(seg(3 * g, g), tb_ref, 8) * (LOG2E * HEAD_DIM ** -0.5)).astype(BF16)
    kb_ref[...] = _rope_apply(seg(4 * g, g), tb_ref, 8).astype(BF16)
    vb_ref[...] = seg(5 * g, g).astype(BF16)
    hc_ref[...] = seg(6 * g, g)
    hd_ref[...] = seg(7 * g, 512)


def _proj(x2d, w, ta, tb):
    t = x2d.shape[0]
    tm = PROJ_TM
    g = GROUP_WIDTH
    row = lambda n: pl.BlockSpec((tm, n), lambda i: (i, 0))
    tab = pl.BlockSpec((3, tm, LANES), lambda i: (0, i, 0))
    bshape = lambda n, dt: jax.ShapeDtypeStruct((t, n), dt)
    return pl.pallas_call(
        _proj_kernel,
        grid=(t // tm,),
        in_specs=[row(D_MODEL), pl.BlockSpec((D_MODEL, PROJ_COLS), lambda i: (0, 0)), tab, tab],
        out_specs=[row(g)] * 7 + [row(512)],
        out_shape=[bshape(g, BF16)] * 6 + [bshape(g, F32), bshape(512, F32)],
        compiler_params=_cparams("arbitrary"),
        name="proj_rope",
    )(x2d, w, ta, tb)


def _rms(x, g):
    return x * lax.rsqrt(jnp.mean(x * x, axis=-1, keepdims=True) + RMS_EPS) * g


def _mla_prep_kernel(hd_ref, td_ref, gq_ref, gkv_ref, wq_ref, wk_ref, wv_ref,
                     q_ref, k_ref, v_ref):
    hd = hd_ref[...]
    cq = _rms(hd[:, 0:MLA_Q_RANK], gq_ref[...]).astype(BF16)
    ckv = _rms(hd[:, MLA_Q_RANK:MLA_Q_RANK + MLA_KV_RANK], gkv_ref[...]).astype(BF16)
    q = jnp.dot(cq, wq_ref[...], preferred_element_type=F32)
    scale = LOG2E * (MLA_NOPE_DIM + MLA_ROPE_DIM) ** -0.5
    q_ref[...] = (_rope_apply(q, td_ref, MLA_ROPE_DIM // 2) * scale).astype(BF16)
    k_rope = _rope_apply(hd[:, 384:512], td_ref, MLA_ROPE_DIM // 2)
    k = jnp.dot(ckv, wk_ref[...], preferred_element_type=F32)
    k_ref[...] = (k + jnp.concatenate([k_rope] * 4, axis=1)).astype(BF16)
    v_ref[...] = jnp.dot(ckv, wv_ref[...], preferred_element_type=F32).astype(BF16)


def _mla_prep(hd, td, gq, gkv, wq, wk, wv):
    t = hd.shape[0]
    tm = PROJ_TM
    row = lambda n: pl.BlockSpec((tm, n), lambda i: (i, 0))
    full = lambda a: pl.BlockSpec(a.shape, lambda i: (0,) * a.ndim)
    return pl.pallas_call(
        _mla_prep_kernel,
        grid=(t // tm,),
        in_specs=[row(512), pl.BlockSpec((3, tm, LANES), lambda i: (0, i, 0)),
                  full(gq), full(gkv), full(wq), full(wk), full(wv)],
        out_specs=[row(512), row(512), row(GROUP_WIDTH)],
        out_shape=[jax.ShapeDtypeStruct((t, 512), BF16), jax.ShapeDtypeStruct((t, 512), BF16),
                   jax.ShapeDtypeStruct((t, GROUP_WIDTH), BF16)],
        compiler_params=_cparams("arbitrary"),
        name="mla_prep",
    )(hd, td, gq, gkv, wq, wk, wv)


def _flash_sweep(n_chunks, k_chunk, w, v_chunks, widths, s_ref, p_ref, mask_chunk=None):
    n = w.shape[1]
    offs = [sum(widths[:g]) for g in range(len(widths))]
    last = n_chunks - 1

    def scores(c, slot):
        s = jnp.dot(k_chunk(c), w, preferred_element_type=F32)
        if mask_chunk is not None:
            s = jnp.where(mask_chunk(c) > 0.0, s, NEG_INF)
        s_ref[slot] = s
        return jnp.max(s, axis=0, keepdims=True)

    def values(c, slot, accs):
        return tuple(
            acc + jnp.dot(v_chunks[g](c), p_ref[slot, :, offs[g]:offs[g] + widths[g]],
                          preferred_element_type=F32)
            for g, acc in enumerate(accs))

    def step(c, slot, carry):
        m, cmaxes, accs = carry
        m_new = jnp.maximum(m, cmaxes[0])
        alpha = jnp.exp2(m - m_new)
        cmax_new = scores(jnp.minimum(c + SWEEP_AHEAD, last), (slot + SWEEP_AHEAD) % SWEEP_SLOTS)
        accs = values(jnp.maximum(c - 1, 0), (slot + 1) % 2, accs)
        accs = tuple(acc * alpha[:, offs[g]:offs[g] + widths[g]] for g, acc in enumerate(accs))
        p = jnp.exp2(s_ref[slot] - m_new)
        if mask_chunk is not None:
            p = p * mask_chunk(c)
        p_ref[slot % 2] = p.astype(BF16)
        return m_new, cmaxes[1:] + (cmax_new,), accs

    p_ref[1] = jnp.zeros(p_ref.shape[1:], BF16)
    carry = (jnp.full((1, n), NEG_INF, F32),
             tuple(scores(min(i, last), i) for i in range(SWEEP_AHEAD)),
             tuple(jnp.zeros((VT_ROWS, wd), F32) for wd in widths))

    def trip(j, carry):
        for i in range(SWEEP_STEPS):
            carry = step(SWEEP_STEPS * j + i, i % SWEEP_SLOTS, carry)
        return carry

    looped = n_chunks // SWEEP_STEPS * SWEEP_STEPS
    carry = lax.fori_loop(0, n_chunks // SWEEP_STEPS, trip, carry)
    for c in range(looped, n_chunks):
        carry = step(c, c % SWEEP_SLOTS, carry)
    return values(last, last % 2, carry[2])


SWEEP_SLOTS = 4
SWEEP_AHEAD = 2
SWEEP_STEPS = 8


def _sweep_scratch(tk, n):
    return [pltpu.VMEM((SWEEP_SLOTS, tk, n), F32), pltpu.VMEM((2, tk, n), BF16)]


def _diff_kernel(lam_ref, g_ref, qt_ref, k_ref, vt_ref, o_ref, s_ref, p_ref, *, lam_init,
                 n_chunks):
    tq = qt_ref.shape[1]
    tk = vt_ref.shape[3]
    lp = lam_ref[...]
    lam = (jnp.exp(jnp.sum(lp[0:1] * lp[1:2], axis=1, keepdims=True))
           - jnp.exp(jnp.sum(lp[2:3] * lp[3:4], axis=1, keepdims=True)) + lam_init)

    qt = qt_ref[...].astype(F32)
    row = lax.broadcasted_iota(jnp.int32, (LANES, tq), 0)
    cols = [jnp.where((row >= DIFF_QK_DIM * c) & (row < DIFF_QK_DIM * (c + 1)), qt, 0.0)
            for c in range(4)]
    w = jnp.concatenate(cols, axis=1).astype(BF16)

    accs = _flash_sweep(
        n_chunks,
        lambda c: k_ref[pl.ds(pl.multiple_of(c * tk, tk), tk), :],
        w,
        [lambda c: vt_ref[0, c], lambda c: vt_ref[1, c]],
        [2 * tq, 2 * tq], s_ref, p_ref)

    gain = g_ref[...] * (1.0 - lam_init)
    for hl, acc in enumerate(accs):
        o1 = acc[0:HEAD_DIM, 0:tq] / acc[HEAD_DIM:HEAD_DIM + 1, 0:tq]
        o2 = acc[0:HEAD_DIM, tq:2 * tq] / acc[HEAD_DIM:HEAD_DIM + 1, tq:2 * tq]
        o = o1 - lam * o2
        ms = jnp.mean(o * o, axis=0, keepdims=True)
        y = o * lax.rsqrt(ms + RMS_EPS) * gain
        o_ref[HEAD_DIM * hl:HEAD_DIM * (hl + 1), :] = y.astype(BF16)


def _diff_attention(lam_params, subln_g, qt, k, vt, layer_idx):
    b, _, s = qt.shape
    tq, tk = DIFF_TQ, ATT_TK
    nck = s // tk
    lam_init = 0.8 - 0.6 * math.exp(-0.3 * layer_idx)
    kern = functools.partial(_diff_kernel, lam_init=lam_init, n_chunks=nck)
    return pl.pallas_call(
        kern,
        grid=(b, 2, s // tq),
        in_specs=[
            pl.BlockSpec((4, DIFF_QK_DIM), lambda bi, j, qi: (0, 0)),
            pl.BlockSpec((HEAD_DIM, 1), lambda bi, j, qi: (0, 0)),
            pl.BlockSpec((None, LANES, tq), lambda bi, j, qi: (bi, j, qi)),
            pl.BlockSpec((None, s, LANES), lambda bi, j, qi: (bi, 0, j)),
            pl.BlockSpec((None, 2, nck, VT_ROWS, tk), lambda bi, j, qi: (bi, j, 0, 0, 0)),
        ],
        out_specs=pl.BlockSpec((None, LANES, tq), lambda bi, j, qi: (bi, j, qi)),
        out_shape=jax.ShapeDtypeStruct((b, GROUP_WIDTH, s), BF16),
        scratch_shapes=_sweep_scratch(tk, 4 * tq),
        compiler_params=_cparams("arbitrary", "arbitrary", "arbitrary"),
        name="diff_attention",
    )(lam_params, subln_g, qt, k, vt)


def _mla_kernel(qt_ref, k_ref, vt_ref, o_ref, s_ref, p_ref, *, n_chunks):
    tq = qt_ref.shape[1]
    tk = vt_ref.shape[2]
    (acc,) = _flash_sweep(
        n_chunks,
        lambda c: k_ref[pl.ds(pl.multiple_of(c * tk, tk), tk), :],
        qt_ref[...],
        [lambda c: vt_ref[c]],
        [tq], s_ref, p_ref)
    o_ref[...] = (acc[0:HEAD_DIM] / acc[HEAD_DIM:HEAD_DIM + 1]).astype(BF16)


def _mla_attention(qt, k, vt):
    b, _, s = qt.shape
    tq, tk = MLA_TQ, ATT_TK
    nck = s // tk
    return pl.pallas_call(
        functools.partial(_mla_kernel, n_chunks=nck),
        grid=(b, 4, s // tq),
        in_specs=[
            pl.BlockSpec((None, LANES, tq), lambda bi, h, qi: (bi, h, qi)),
            pl.BlockSpec((None, s, LANES), lambda bi, h, qi: (bi, 0, h)),
            pl.BlockSpec((None, None, nck, VT_ROWS, tk), lambda bi, h, qi: (bi, h, 0, 0, 0)),
        ],
        out_specs=pl.BlockSpec((None, HEAD_DIM, tq), lambda bi, h, qi: (bi, h, qi)),
        out_shape=jax.ShapeDtypeStruct((b, GROUP_WIDTH, s), BF16),
        scratch_shapes=_sweep_scratch(tk, tq),
        compiler_params=_cparams("arbitrary", "arbitrary", "arbitrary"),
        name="mla_attention",
    )(qt, k, vt)


def _dil_multiplicity(tq, tk):
    n_win = (tq + 2 * DIL_HALO) // tk
    key = jnp.arange(n_win * tk)[:, None] - DIL_HALO
    delta = key - jnp.arange(tq)[None, :]
    mult = jnp.zeros(delta.shape, F32)
    for window, dil in DIL_PATTERNS:
        reach = (window // 2 // dil) * dil
        mult = mult + ((delta % dil == 0) & (jnp.abs(delta) <= reach)).astype(F32)
    return mult.reshape(n_win, tk, tq)


def _dil_kernel(c_ref, qt_ref, k_ref, vt_ref, o_ref, s_ref, p_ref, *, n_seq_chunks):
    tq = qt_ref.shape[1]
    n_win, tk = c_ref.shape[0], c_ref.shape[1]
    halo_chunks = DIL_HALO // tk
    qi = pl.program_id(2)

    qt = qt_ref[...].astype(F32)
    row = lax.broadcasted_iota(jnp.int32, (LANES, tq), 0)
    w = jnp.concatenate([jnp.where(row < HEAD_DIM, qt, 0.0),
                         jnp.where(row >= HEAD_DIM, qt, 0.0)], axis=1).astype(BF16)

    def mask_chunk(c):
        ck = qi * (tq // tk) + c
        inside = jnp.where((ck >= halo_chunks) & (ck < halo_chunks + n_seq_chunks), 1.0, 0.0)
        mult = c_ref[c] * inside
        return jnp.concatenate([mult, mult], axis=1)

    def k_chunk(c):
        ck = qi * (tq // tk) + c
        return k_ref[pl.ds(pl.multiple_of(ck * tk, tk), tk), :]

    accs = _flash_sweep(
        n_win, k_chunk, w,
        [lambda c: vt_ref[0, qi * (tq // tk) + c], lambda c: vt_ref[1, qi * (tq // tk) + c]],
        [tq, tq], s_ref, p_ref, mask_chunk=mask_chunk)
    for hl, acc in enumerate(accs):
        o_ref[HEAD_DIM * hl:HEAD_DIM * (hl + 1), :] = (
            acc[0:HEAD_DIM] / acc[HEAD_DIM:HEAD_DIM + 1]).astype(BF16)


def _dil_attention(mult, qt, k_pad, vt_pad):
    b, _, s = qt.shape
    tq, tk = DIL_TQ, ATT_TK
    n_win = mult.shape[0]
    sp = s + 2 * DIL_HALO
    return pl.pallas_call(
        functools.partial(_dil_kernel, n_seq_chunks=s // tk),
        grid=(b, 2, s // tq),
        in_specs=[
            pl.BlockSpec((n_win, tk, tq), lambda bi, j, qi: (0, 0, 0)),
            pl.BlockSpec((None, LANES, tq), lambda bi, j, qi: (bi, j, qi)),
            pl.BlockSpec((None, sp, LANES), lambda bi, j, qi: (bi, 0, j)),
            pl.BlockSpec((None, 2, sp // tk, VT_ROWS, tk), lambda bi, j, qi: (bi, j, 0, 0, 0)),
        ],
        out_specs=pl.BlockSpec((None, LANES, tq), lambda bi, j, qi: (bi, j, qi)),
        out_shape=jax.ShapeDtypeStruct((b, GROUP_WIDTH, s), BF16),
        scratch_shapes=_sweep_scratch(tk, 2 * tq),
        compiler_params=_cparams("arbitrary", "arbitrary", "arbitrary"),
        name="dilated_attention",
    )(mult, qt, k_pad, vt_pad)


def _with_halo(prev_ref, cur_ref, next_ref, i, n_tiles):
    prev = jnp.where(i > 0, prev_ref[...], 0.0)
    nxt = jnp.where(i < n_tiles - 1, next_ref[...], 0.0)
    return jnp.concatenate([prev, cur_ref[...], nxt], axis=0)


def _shift_rows(x, k):
    n = x.shape[0]
    return pltpu.roll(x, (n - k) % n, 0)


def _pool_kernel(prev_ref, cur_ref, next_ref, w_ref, scale_ref, o_ref, *, seq_len):
    tm = cur_ref.shape[0]
    i = pl.program_id(1)
    x = _with_halo(prev_ref, cur_ref, next_ref, i, seq_len // tm)
    t = i * tm + lax.broadcasted_iota(jnp.int32, (tm, 1), 0)
    lane_group = lax.broadcasted_iota(jnp.int32, (tm, GROUP_WIDTH), 1) // POOL_GROUP

    run = x
    mean = None
    for g, wnd in enumerate(POOL_WINDOWS):
        run = run + _shift_rows(run, wnd // 2)
        total = _shift_rows(run, -(wnd // 2))[HALO:HALO + tm]
        lo = jnp.clip(t - wnd // 2, 0, seq_len - 1)
        hi = jnp.clip(t + wnd - wnd // 2 - 1, 0, seq_len - 1)
        cand = total / (hi - lo + 1).astype(F32)
        mean = cand if mean is None else jnp.where(lane_group == g, cand, mean)
    d = (mean - cur_ref[...]).astype(BF16)
    y = jnp.dot(d, w_ref[...], preferred_element_type=F32) * scale_ref[...]
    o_ref[...] = y.astype(BF16)


def _halo_specs(tm, cols):
    nb = tm // HALO
    return [
        pl.BlockSpec((None, HALO, cols), lambda bi, i: (bi, jnp.maximum(i * nb - 1, 0), 0)),
        pl.BlockSpec((None, tm, cols), lambda bi, i: (bi, i, 0)),
        pl.BlockSpec((None, HALO, cols), lambda bi, i, nb=nb: (bi, (i + 1) * nb, 0)),
    ]


def _pool_mixer(hc3, w_bd, scale):
    b, s, c = hc3.shape
    tm = POOL_TM
    prev_s, cur_s, next_s = _halo_specs(tm, c)
    last = s // HALO - 1
    next_s = pl.BlockSpec((None, HALO, c),
                          lambda bi, i: (bi, jnp.minimum((i + 1) * (tm // HALO), last), 0))
    return pl.pallas_call(
        functools.partial(_pool_kernel, seq_len=s),
        grid=(b, s // tm),
        in_specs=[prev_s, cur_s, next_s,
                  pl.BlockSpec(w_bd.shape, lambda bi, i: (0, 0)),
                  pl.BlockSpec(scale.shape, lambda bi, i: (0, 0))],
        out_specs=pl.BlockSpec((None, tm, c), lambda bi, i: (bi, i, 0)),
        out_shape=jax.ShapeDtypeStruct((b, s, c), BF16),
        compiler_params=_cparams("arbitrary", "arbitrary"),
        name="pool_mixer",
    )(hc3, hc3, hc3, w_bd, scale)


def _layer_norm(z, g, b):
    mu = jnp.mean(z, axis=-1, keepdims=True)
    zc = z - mu
    var = jnp.mean(zc * zc, axis=-1, keepdims=True)
    return zc * lax.rsqrt(var + LN_EPS) * g + b


def _tdot(at, w):
    return lax.dot_general(at, w, (((0,), (0,)), ((), ())), preferred_element_type=F32)


def _out_kernel(x_ref, ya_ref, yb_ref, yc_ref, yd_ref, w_ref, g_ref, b_ref, o_ref):
    gw = GROUP_WIDTH
    z = _tdot(ya_ref[...], w_ref[0:gw, :])
    z = z + _tdot(yb_ref[...], w_ref[gw:2 * gw, :])
    z = z + jnp.dot(yc_ref[...], w_ref[2 * gw:3 * gw, :], preferred_element_type=F32)
    z = z + _tdot(yd_ref[...], w_ref[3 * gw:4 * gw, :])
    o_ref[...] = _layer_norm(DN_ALPHA * x_ref[...] + z, g_ref[...], b_ref[...])


def _out_proj(x3, ya_t, yb_t, yc, yd_t, w, g, bias):
    b, s, d = x3.shape
    tm = OUT_TM
    tok = lambda n: pl.BlockSpec((None, tm, n), lambda bi, i: (bi, i, 0))
    feat = pl.BlockSpec((None, GROUP_WIDTH, tm), lambda bi, i: (bi, 0, i))
    full = lambda a: pl.BlockSpec(a.shape, lambda bi, i: (0,) * a.ndim)
    return pl.pallas_call(
        _out_kernel,
        grid=(b, s // tm),
        in_specs=[tok(d), feat, feat, tok(GROUP_WIDTH), feat, full(w), full(g), full(bias)],
        out_specs=tok(d),
        out_shape=jax.ShapeDtypeStruct((b, s, d), F32),
        compiler_params=_cparams("arbitrary", "arbitrary"),
        name="out_proj_ln",
    )(x3, ya_t, yb_t, yc, yd_t, w, g, bias)


def _ffn_kernel(prev_ref, cur_ref, next_ref, wg_ref, wu_ref, wd_ref, cg_ref, cu_ref,
                g_ref, b_ref, o_ref, acc_ref, *, seq_len):
    tm = cur_ref.shape[0]
    i = pl.program_id(1)
    xe = _with_halo(prev_ref, cur_ref, next_ref, i, seq_len // tm).astype(BF16)

    def conv(u, c):
        y = _shift_rows(u, -1) * c[0:1] + u * c[1:2] + _shift_rows(u, 1) * c[2:3] + c[3:4]
        return y[HALO:HALO + tm]

    acc_ref[...] = jnp.zeros_like(acc_ref)

    def body(c, carry):
        gate = conv(jnp.dot(xe, wg_ref[c], preferred_element_type=F32), cg_ref[c])
        up = conv(jnp.dot(xe, wu_ref[c], preferred_element_type=F32), cu_ref[c])
        act = (gate * jax.nn.sigmoid(gate) * up).astype(BF16)
        acc_ref[...] += jnp.dot(act, wd_ref[c], preferred_element_type=F32)
        return carry

    lax.fori_loop(0, wg_ref.shape[0], body, 0)
    o_ref[...] = _layer_norm(DN_ALPHA * cur_ref[...] + acc_ref[...], g_ref[...], b_ref[...])


def _ffn(x3, wg, wu, wd, cg, cu, g, bias):
    b, s, d = x3.shape
    tm = FFN_TM
    prev_s, cur_s, _ = _halo_specs(tm, d)
    last = s // HALO - 1
    next_s = pl.BlockSpec((None, HALO, d),
                          lambda bi, i: (bi, jnp.minimum((i + 1) * (tm // HALO), last), 0))
    full = lambda a: pl.BlockSpec(a.shape, lambda bi, i: (0,) * a.ndim)
    return pl.pallas_call(
        functools.partial(_ffn_kernel, seq_len=s),
        grid=(b, s // tm),
        in_specs=[prev_s, cur_s, next_s, full(wg), full(wu), full(wd), full(cg), full(cu),
                  full(g), full(bias)],
        out_specs=pl.BlockSpec((None, tm, d), lambda bi, i: (bi, i, 0)),
        out_shape=jax.ShapeDtypeStruct((b, s, d), F32),
        scratch_shapes=[pltpu.VMEM((tm, d), F32)],
        compiler_params=_cparams("arbitrary", "arbitrary"),
        name="conv_ffn_ln",
    )(x3, x3, x3, wg, wu, wd, cg, cu, g, bias)


def _feature_major(a, b, s):
    return a.reshape(b, s, a.shape[-1]).transpose(0, 2, 1)


def _values_t(v, b, s, tk, pad=0):
    vt = v.reshape(b, s, 4, HEAD_DIM).transpose(0, 2, 3, 1)
    vt = jnp.concatenate([vt, jnp.ones((b, 4, SUM_ROWS, s), v.dtype)], axis=2)
    if pad:
        vt = jnp.pad(vt, ((0, 0), (0, 0), (0, 0), (pad, pad)))
    sp = s + 2 * pad
    return vt.reshape(b, 4, VT_ROWS, sp // tk, tk).transpose(0, 1, 3, 2, 4)


def _pad_cols(w, groups, width, padded):
    k = w.shape[0]
    w = w.reshape(k, groups, width)
    return jnp.pad(w, ((0, 0), (0, 0), (0, padded - width))).reshape(k, groups * padded)


def kernel(x, positions, w_in, diff_lambda, diff_subln, pool_w, pool_scale, mla_q_norm,
           mla_kv_norm, mla_w_uq, mla_w_ukv, w_out, ln1_g, ln1_b, ffn_w_up, ffn_conv_w,
           ffn_conv_b, ffn_w_down, ln2_g, ln2_b):
    b, s, d = x.shape
    t = b * s
    assert d == D_MODEL and s % 1024 == 0

    consts = jnp.stack([
        _rope_consts(DIFF_QK_DIM, DIFF_QK_DIM // 4, 0),
        _rope_consts(HEAD_DIM, HEAD_DIM // 4, 0),
        _rope_consts(LANES, MLA_ROPE_DIM, MLA_NOPE_DIM),
    ])
    tables = _rope_tables(positions.reshape(t, 1), consts)
    ta, tb, td = tables[0], tables[1], tables[2]
    mult = _dil_multiplicity(DIL_TQ, ATT_TK)

    n_ch = D_FF // FFN_CHUNK
    for l in range(DEPTH):
        wi = w_in[l]
        w_proj = jnp.concatenate([
            wi[:, :7 * GROUP_WIDTH + MLA_Q_RANK + MLA_KV_RANK],
            jnp.zeros((d, MLA_NOPE_DIM), wi.dtype),
            wi[:, 7 * GROUP_WIDTH + MLA_Q_RANK + MLA_KV_RANK:],
            jnp.zeros((d, LANES - MLA_NOPE_DIM - MLA_ROPE_DIM), wi.dtype)], axis=1).astype(BF16)
        wq = _pad_cols(mla_w_uq[l], 4, MLA_NOPE_DIM + MLA_ROPE_DIM, LANES).astype(BF16)
        wkv = mla_w_ukv[l].reshape(MLA_KV_RANK, 4, 2, HEAD_DIM)
        wk = _pad_cols(wkv[:, :, 0].reshape(MLA_KV_RANK, 4 * HEAD_DIM), 4, HEAD_DIM, LANES).astype(BF16)
        wv = wkv[:, :, 1].reshape(MLA_KV_RANK, 4 * HEAD_DIM).astype(BF16)
        pw = pool_w[l]
        w_pool = jnp.zeros((GROUP_WIDTH, GROUP_WIDTH), pw.dtype)
        for g in range(4):
            w_pool = w_pool.at[g * POOL_GROUP:(g + 1) * POOL_GROUP,
                               g * POOL_GROUP:(g + 1) * POOL_GROUP].set(pw[g])
        w_pool = w_pool.astype(BF16)
        wup = ffn_w_up[l]
        wg_c = wup[:, :D_FF].reshape(d, n_ch, FFN_CHUNK).transpose(1, 0, 2).astype(BF16)
        wu_c = wup[:, D_FF:].reshape(d, n_ch, FFN_CHUNK).transpose(1, 0, 2).astype(BF16)
        wd_c = ffn_w_down[l].reshape(n_ch, FFN_CHUNK, d).astype(BF16)
        conv = jnp.concatenate([ffn_conv_w[l], ffn_conv_b[l][None, :],
                                jnp.zeros((4, 2 * D_FF), F32)], axis=0)
        cg_c = conv[:, :D_FF].reshape(8, n_ch, FFN_CHUNK).transpose(1, 0, 2)
        cu_c = conv[:, D_FF:].reshape(8, n_ch, FFN_CHUNK).transpose(1, 0, 2)

        x2d = x.reshape(t, d)
        qa, ka, va, qb, kb, vb, hc, hd = _proj(x2d, w_proj, ta, tb)
        qd, kd, vd = _mla_prep(hd, td, mla_q_norm[l][None, :], mla_kv_norm[l][None, :], wq, wk, wv)

        ya_t = _diff_attention(diff_lambda[l], diff_subln[l][:, None],
                               _feature_major(qa, b, s), ka.reshape(b, s, GROUP_WIDTH),
                               _values_t(va, b, s, ATT_TK), l)
        kb_pad = jnp.pad(kb.reshape(b, s, GROUP_WIDTH), ((0, 0), (DIL_HALO, DIL_HALO), (0, 0)))
        yb_t = _dil_attention(mult, _feature_major(qb, b, s), kb_pad,
                              _values_t(vb, b, s, ATT_TK, pad=DIL_HALO))
        yc = _pool_mixer(hc.reshape(b, s, GROUP_WIDTH), w_pool, pool_scale[l][None, :])
        yd_t = _mla_attention(_feature_major(qd, b, s), kd.reshape(b, s, 512),
                              _values_t(vd, b, s, ATT_TK))

        x = _out_proj(x, ya_t, yb_t, yc, yd_t, w_out[l].astype(BF16),
                      ln1_g[l][None, :], ln1_b[l][None, :])
        x = _ffn(x, wg_c, wu_c, wd_c, cg_c, cu_c, ln2_g[l][None, :], ln2_b[l][None, :])
    return x
```

```python
import functools
import math

import jax
import jax.numpy as jnp
from jax import lax
from jax.experimental import pallas as pl
from jax.experimental.pallas import tpu as pltpu

F32 = jnp.float32
BF16 = jnp.bfloat16

D_MODEL = 1024
DEPTH = 4
HEAD_DIM = 64
GROUP_WIDTH = 256
ROPE_THETA = 500000.0
NEG_INF = -1e30
LOG2E = math.log2(math.e)

DIFF_QK_DIM = 32
DIL_PATTERNS = ((128, 1), (512, 4), (2048, 16))
POOL_WINDOWS = (2, 4, 8, 16)
POOL_GROUP = 64
MLA_Q_RANK = 256
MLA_KV_RANK = 128
MLA_NOPE_DIM = 64
MLA_ROPE_DIM = 32
D_FF = 2816
DN_ALPHA = (2 * DEPTH) ** 0.25
LN_EPS = 1e-5
RMS_EPS = 1e-6

LANES = 128
SUM_ROWS = 16
VT_ROWS = HEAD_DIM + SUM_ROWS
DIL_HALO = 1024
VMEM_LIMIT = 56 * 1024 * 1024

PROJ_TM = 512
ATT_TK = 512
DIFF_TQ = 256
MLA_TQ = 512
DIL_TQ = 256
DIL_TK = 256
POOL_TM = 512
OUT_TM = 512
FFN_TM = 512
FFN_CHUNK = 256
HALO = 8


def _cparams(*sem):
    return pltpu.CompilerParams(dimension_semantics=sem, vmem_limit_bytes=VMEM_LIMIT)


def _rope_tables_kernel(pos_ref, c_ref, o_ref):
    pos = pos_ref[...].astype(F32)
    ang = pos * c_ref[0:1, :]
    cs = jnp.cos(ang)
    sn = jnp.sin(ang)
    is_rope = c_ref[1:2, :]
    o_ref[0] = cs * is_rope + (1.0 - is_rope)
    o_ref[1] = -sn * c_ref[2:3, :]
    o_ref[2] = sn * c_ref[3:4, :]


def _rope_consts(period, rot_dim, lane0):
    half = rot_dim // 2
    inv_freq = ROPE_THETA ** (-jnp.arange(half, dtype=F32) * 2.0 / rot_dim)
    lane = jnp.arange(LANES)
    d = lane % period - lane0
    is_rope = (d >= 0) & (d < rot_dim)
    f_idx = jnp.clip(d, 0, rot_dim - 1) % half
    rows = [
        jnp.where(is_rope, inv_freq[f_idx], 0.0),
        is_rope.astype(F32),
        (is_rope & (d < half)).astype(F32),
        (is_rope & (d >= half)).astype(F32),
    ]
    rows += [jnp.zeros((LANES,), F32)] * 4
    return jnp.stack(rows).astype(F32)


def _rope_tables(pos2d, consts):
    t = pos2d.shape[0]
    tm = 1024
    n = consts.shape[0]
    return pl.pallas_call(
        _rope_tables_kernel,
        grid=(n, t // tm),
        in_specs=[
            pl.BlockSpec((tm, 1), lambda j, i: (i, 0)),
            pl.BlockSpec((None, 8, LANES), lambda j, i: (j, 0, 0)),
        ],
        out_specs=pl.BlockSpec((None, 3, tm, LANES), lambda j, i: (j, 0, i, 0)),
        out_shape=jax.ShapeDtypeStruct((n, 3, t, LANES), F32),
        compiler_params=_cparams("arbitrary", "arbitrary"),
        name="rope_tables",
    )(pos2d, consts)


def _rope_apply(h, t_ref, half):
    cos, sa, sb = t_ref[0], t_ref[1], t_ref[2]
    outs = []
    for j in range(h.shape[1] // LANES):
        xj = h[:, LANES * j:LANES * (j + 1)]
        fwd = pltpu.roll(xj, LANES - half, 1)
        bwd = pltpu.roll(xj, half, 1)
        outs.append(xj * cos + fwd * sa + bwd * sb)
    return outs[0] if len(outs) == 1 else jnp.concatenate(outs, axis=1)


PROJ_COLS = 6 * GROUP_WIDTH + GROUP_WIDTH + 512


def _proj_kernel(x_ref, w_ref, ta_ref, tb_ref,
                 qa_ref, ka_ref, va_ref, qb_ref, kb_ref, vb_ref, hc_ref, hd_ref):
    x = x_ref[...].astype(BF16)

    def seg(j0, n):
        return jnp.dot(x, w_ref[:, j0:j0 + n], preferred_element_type=F32)

    g = GROUP_WIDTH
    qa = _rope_apply(seg(0, g), ta_ref, 4) * (LOG2E * DIFF_QK_DIM ** -0.5)
    qa_ref[...] = qa.T.astype(BF16)
    ka_ref[...] = _rope_apply(seg(g, g), ta_ref, 4).astype(BF16)
    _store_values_t(seg(2 * g, g), va_ref)
    qb = _rope_apply(seg(3 * g, g), tb_ref, 8) * (LOG2E * HEAD_DIM ** -0.5)
    qb_ref[...] = qb.T.astype(BF16)
    kb_ref[...] = _rope_apply(seg(4 * g, g), tb_ref, 8).astype(BF16)
    _store_values_t(seg(5 * g, g), vb_ref)
    hc_ref[...] = seg(6 * g, g)
    hd_ref[...] = seg(7 * g, 512)


def _store_values_t(v, vt_ref):
    tk = vt_ref.shape[3]
    vt = v.astype(BF16).T
    ones = jnp.ones((SUM_ROWS, tk), BF16)
    for h in range(4):
        for c in range(vt_ref.shape[1]):
            vt_ref[h, c, 0:HEAD_DIM, :] = vt[HEAD_DIM * h:HEAD_DIM * (h + 1), tk * c:tk * (c + 1)]
            vt_ref[h, c, HEAD_DIM:VT_ROWS, :] = ones


def _token_spec(tm, n):
    return pl.BlockSpec((None, tm, n), lambda bi, i: (bi, i, 0))


def _feature_spec(tm, n):
    return pl.BlockSpec((None, n, tm), lambda bi, i: (bi, 0, i))


def _values_spec(tm, tk):
    return pl.BlockSpec((None, 4, tm // tk, VT_ROWS, tk), lambda bi, i: (bi, 0, i, 0, 0))


def _values_shape(b, s, tk):
    return jax.ShapeDtypeStruct((b, 4, s // tk, VT_ROWS, tk), BF16)


def _table_spec(tm, s):
    return pl.BlockSpec((3, tm, LANES), lambda bi, i: (0, bi * (s // tm) + i, 0))


def _proj(x3, w, ta, tb):
    b, s, d = x3.shape
    tm = PROJ_TM
    g = GROUP_WIDTH
    tok = lambda n, dt: jax.ShapeDtypeStruct((b, s, n), dt)
    feat = jax.ShapeDtypeStruct((b, g, s), BF16)
    return pl.pallas_call(
        _proj_kernel,
        grid=(b, s // tm),
        in_specs=[_token_spec(tm, d), pl.BlockSpec((d, PROJ_COLS), lambda bi, i: (0, 0)),
                  _table_spec(tm, s), _table_spec(tm, s)],
        out_specs=[_feature_spec(tm, g), _token_spec(tm, g), _values_spec(tm, ATT_TK),
                   _feature_spec(tm, g), _token_spec(tm, g), _values_spec(tm, DIL_TK),
                   _token_spec(tm, g), _token_spec(tm, 512)],
        out_shape=[feat, tok(g, BF16), _values_shape(b, s, ATT_TK),
                   feat, tok(g, BF16), _values_shape(b, s, DIL_TK),
                   tok(g, F32), tok(512, F32)],
        compiler_params=_cparams("arbitrary", "arbitrary"),
        name="proj_rope",
    )(x3, w, ta, tb)


def _rms(x, g):
    return x * lax.rsqrt(jnp.mean(x * x, axis=-1, keepdims=True) + RMS_EPS) * g


def _mla_prep_kernel(hd_ref, td_ref, gq_ref, gkv_ref, wq_ref, wk_ref, wv_ref,
                     q_ref, k_ref, v_ref):
    hd = hd_ref[...]
    cq = _rms(hd[:, 0:MLA_Q_RANK], gq_ref[...]).astype(BF16)
    ckv = _rms(hd[:, MLA_Q_RANK:MLA_Q_RANK + MLA_KV_RANK], gkv_ref[...]).astype(BF16)
    q = jnp.dot(cq, wq_ref[...], preferred_element_type=F32)
    scale = LOG2E * (MLA_NOPE_DIM + MLA_ROPE_DIM) ** -0.5
    q_ref[...] = (_rope_apply(q, td_ref, MLA_ROPE_DIM // 2) * scale).T.astype(BF16)
    k_rope = _rope_apply(hd[:, 384:512], td_ref, MLA_ROPE_DIM // 2)
    k = jnp.dot(ckv, wk_ref[...], preferred_element_type=F32)
    k_ref[...] = (k + jnp.concatenate([k_rope] * 4, axis=1)).astype(BF16)
    _store_values_t(jnp.dot(ckv, wv_ref[...], preferred_element_type=F32), v_ref)


def _mla_prep(hd, td, gq, gkv, wq, wk, wv):
    b, s, _ = hd.shape
    tm = PROJ_TM
    full = lambda a: pl.BlockSpec(a.shape, lambda bi, i: (0,) * a.ndim)
    return pl.pallas_call(
        _mla_prep_kernel,
        grid=(b, s // tm),
        in_specs=[_token_spec(tm, 512), _table_spec(tm, s),
                  full(gq), full(gkv), full(wq), full(wk), full(wv)],
        out_specs=[_feature_spec(tm, 512), _token_spec(tm, 512), _values_spec(tm, ATT_TK)],
        out_shape=[jax.ShapeDtypeStruct((b, 512, s), BF16),
                   jax.ShapeDtypeStruct((b, s, 512), BF16), _values_shape(b, s, ATT_TK)],
        compiler_params=_cparams("arbitrary", "arbitrary"),
        name="mla_prep",
    )(hd, td, gq, gkv, wq, wk, wv)


def _flash_sweep(n_chunks, k_chunk, w, v_chunks, widths, s_ref, p_ref, mask_chunk=None):
    n = w.shape[1]
    offs = [sum(widths[:g]) for g in range(len(widths))]
    last = n_chunks - 1

    def scores(c, slot):
        s = jnp.dot(k_chunk(c), w, preferred_element_type=F32)
        if mask_chunk is not None:
            s = jnp.where(mask_chunk(c) > 0.0, s, NEG_INF)
        s_ref[slot] = s
        return jnp.max(s, axis=0, keepdims=True)

    def values(c, slot, accs):
        return tuple(
            acc + jnp.dot(v_chunks[g](c), p_ref[slot, :, offs[g]:offs[g] + widths[g]],
                          preferred_element_type=F32)
            for g, acc in enumerate(accs))

    def step(c, slot, carry):
        m, cmaxes, accs = carry
        m_new = jnp.maximum(m, cmaxes[0])
        alpha = jnp.exp2(m - m_new)
        cmax_new = scores(jnp.minimum(c + SWEEP_AHEAD, last), (slot + SWEEP_AHEAD) % SWEEP_SLOTS)
        accs = values(jnp.maximum(c - 1, 0), (slot + 1) % 2, accs)
        accs = tuple(acc * alpha[:, offs[g]:offs[g] + widths[g]] for g, acc in enumerate(accs))
        p = jnp.exp2(s_ref[slot] - m_new)
        if mask_chunk is not None:
            p = p * mask_chunk(c)
        p_ref[slot % 2] = p.astype(BF16)
        return m_new, cmaxes[1:] + (cmax_new,), accs

    p_ref[1] = jnp.zeros(p_ref.shape[1:], BF16)
    carry = (jnp.full((1, n), NEG_INF, F32),
             tuple(scores(min(i, last), i) for i in range(SWEEP_AHEAD)),
             tuple(jnp.zeros((VT_ROWS, wd), F32) for wd in widths))

    def trip(j, carry):
        for i in range(SWEEP_STEPS):
            carry = step(SWEEP_STEPS * j + i, i % SWEEP_SLOTS, carry)
        return carry

    looped = n_chunks // SWEEP_STEPS * SWEEP_STEPS
    carry = lax.fori_loop(0, n_chunks // SWEEP_STEPS, trip, carry)
    for c in range(looped, n_chunks):
        carry = step(c, c % SWEEP_SLOTS, carry)
    return values(last, last % 2, carry[2])


SWEEP_SLOTS = 4
SWEEP_AHEAD = 2
SWEEP_STEPS = 8


def _sweep_scratch(tk, n):
    return [pltpu.VMEM((SWEEP_SLOTS, tk, n), F32), pltpu.VMEM((2, tk, n), BF16)]


SAFE_LOG2_SPAN = 100.0
KMAX_ROWS = 1024


def _key_absmax(k_ref, kmax_ref, refresh):
    @pl.when(refresh)
    def _():
        def body(i, m):
            blk = k_ref[pl.ds(pl.multiple_of(i * KMAX_ROWS, KMAX_ROWS), KMAX_ROWS), :]
            return jnp.maximum(m, jnp.max(jnp.abs(blk.astype(F32)), axis=0, keepdims=True))
        m = lax.fori_loop(0, k_ref.shape[0] // KMAX_ROWS, body, jnp.zeros((1, LANES), F32))
        kmax_ref[...] = jnp.broadcast_to(m, kmax_ref.shape)


def _fixed_sweep(n_chunks, k_chunk, w, v_chunks, widths, s_ref, p_ref, m_ref):
    offs = [sum(widths[:g]) for g in range(len(widths))]

    def probs(s, slot):
        p_ref[slot] = jnp.exp2(s - m_ref).astype(BF16)

    def values(c, slot, accs):
        return tuple(
            acc + jnp.dot(v_chunks[g](c), p_ref[slot, :, offs[g]:offs[g] + widths[g]],
                          preferred_element_type=F32)
            for g, acc in enumerate(accs))

    def step(c, slot, accs):
        s = jnp.dot(k_chunk(c), w, preferred_element_type=F32)
        accs = values(c - 1, 1 - slot, accs)
        probs(s, slot)
        return accs

    probs(s_ref[0], 0)
    accs = tuple(jnp.zeros((VT_ROWS, wd), F32) for wd in widths)

    def trip(j, accs):
        for i in range(SWEEP_STEPS):
            accs = step(1 + SWEEP_STEPS * j + i, (1 + i) % 2, accs)
        return accs

    trips = (n_chunks - 1) // SWEEP_STEPS
    accs = lax.fori_loop(0, trips, trip, accs)
    for c in range(1 + trips * SWEEP_STEPS, n_chunks):
        accs = step(c, c % 2, accs)
    return values(n_chunks - 1, (n_chunks - 1) % 2, accs)


def _dense_sweep(n_chunks, k_chunk, w, v_chunks, widths, s_ref, p_ref, kmax_ref):
    s0 = jnp.dot(k_chunk(0), w, preferred_element_type=F32)
    s_ref[0] = s0
    m0 = jnp.max(s0, axis=0, keepdims=True)
    kmax = (kmax_ref[...] * 1.01).astype(BF16)
    bound = jnp.dot(kmax, jnp.abs(w), preferred_element_type=F32)[0:1]
    safe = jnp.max(bound - m0) <= SAFE_LOG2_SPAN
    return lax.cond(
        safe,
        lambda: _fixed_sweep(n_chunks, k_chunk, w, v_chunks, widths, s_ref, p_ref, m0),
        lambda: _flash_sweep(n_chunks, k_chunk, w, v_chunks, widths, s_ref, p_ref))


def _diff_kernel(lam_ref, g_ref, qt_ref, k_ref, vt_ref, o_ref, s_ref, p_ref, kmax_ref, *,
                 lam_init, n_chunks):
    tq = qt_ref.shape[1]
    tk = vt_ref.shape[3]
    lp = lam_ref[...]
    lam = (jnp.exp(jnp.sum(lp[0:1] * lp[1:2], axis=1, keepdims=True))
           - jnp.exp(jnp.sum(lp[2:3] * lp[3:4], axis=1, keepdims=True)) + lam_init)

    qt = qt_ref[...].astype(F32)
    row = lax.broadcasted_iota(jnp.int32, (LANES, tq), 0)
    cols = [jnp.where((row >= DIFF_QK_DIM * c) & (row < DIFF_QK_DIM * (c + 1)), qt, 0.0)
            for c in range(4)]
    w = jnp.concatenate(cols, axis=1).astype(BF16)

    _key_absmax(k_ref, kmax_ref, pl.program_id(2) == 0)
    accs = _dense_sweep(
        n_chunks,
        lambda c: k_ref[pl.ds(pl.multiple_of(c * tk, tk), tk), :],
        w,
        [lambda c: vt_ref[0, c], lambda c: vt_ref[1, c]],
        [2 * tq, 2 * tq], s_ref, p_ref, kmax_ref)

    gain = g_ref[...] * (1.0 - lam_init)
    for hl, acc in enumerate(accs):
        o1 = acc[0:HEAD_DIM, 0:tq] / acc[HEAD_DIM:HEAD_DIM + 1, 0:tq]
        o2 = acc[0:HEAD_DIM, tq:2 * tq] / acc[HEAD_DIM:HEAD_DIM + 1, tq:2 * tq]
        o = o1 - lam * o2
        ms = jnp.mean(o * o, axis=0, keepdims=True)
        y = o * lax.rsqrt(ms + RMS_EPS) * gain
        o_ref[HEAD_DIM * hl:HEAD_DIM * (hl + 1), :] = y.astype(BF16)


def _diff_attention(lam_params, subln_g, qt, k, vt, layer_idx):
    b, _, s = qt.shape
    tq, tk = DIFF_TQ, ATT_TK
    nck = s // tk
    lam_init = 0.8 - 0.6 * math.exp(-0.3 * layer_idx)
    kern = functools.partial(_diff_kernel, lam_init=lam_init, n_chunks=nck)
    return pl.pallas_call(
        kern,
        grid=(b, 2, s // tq),
        in_specs=[
            pl.BlockSpec((4, DIFF_QK_DIM), lambda bi, j, qi: (0, 0)),
            pl.BlockSpec((HEAD_DIM, 1), lambda bi, j, qi: (0, 0)),
            pl.BlockSpec((None, LANES, tq), lambda bi, j, qi: (bi, j, qi)),
            pl.BlockSpec((None, s, LANES), lambda bi, j, qi: (bi, 0, j)),
            pl.BlockSpec((None, 2, nck, VT_ROWS, tk), lambda bi, j, qi: (bi, j, 0, 0, 0)),
        ],
        out_specs=pl.BlockSpec((None, LANES, tq), lambda bi, j, qi: (bi, j, qi)),
        out_shape=jax.ShapeDtypeStruct((b, GROUP_WIDTH, s), BF16),
        scratch_shapes=_sweep_scratch(tk, 4 * tq) + [pltpu.VMEM((8, LANES), F32)],
        compiler_params=_cparams("arbitrary", "arbitrary", "arbitrary"),
        name="diff_attention",
    )(lam_params, subln_g, qt, k, vt)


def _mla_kernel(qt_ref, k_ref, vt_ref, o_ref, s_ref, p_ref, kmax_ref, *, n_chunks):
    tq = qt_ref.shape[1]
    tk = vt_ref.shape[2]
    _key_absmax(k_ref, kmax_ref, pl.program_id(2) == 0)
    (acc,) = _dense_sweep(
        n_chunks,
        lambda c: k_ref[pl.ds(pl.multiple_of(c * tk, tk), tk), :],
        qt_ref[...],
        [lambda c: vt_ref[c]],
        [tq], s_ref, p_ref, kmax_ref)
    o_ref[...] = (acc[0:HEAD_DIM] / acc[HEAD_DIM:HEAD_DIM + 1]).astype(BF16)


def _mla_attention(qt, k, vt):
    b, _, s = qt.shape
    tq, tk = MLA_TQ, ATT_TK
    nck = s // tk
    return pl.pallas_call(
        functools.partial(_mla_kernel, n_chunks=nck),
        grid=(b, 4, s // tq),
        in_specs=[
            pl.BlockSpec((None, LANES, tq), lambda bi, h, qi: (bi, h, qi)),
            pl.BlockSpec((None, s, LANES), lambda bi, h, qi: (bi, 0, h)),
            pl.BlockSpec((None, None, nck, VT_ROWS, tk), lambda bi, h, qi: (bi, h, 0, 0, 0)),
        ],
        out_specs=pl.BlockSpec((None, HEAD_DIM, tq), lambda bi, h, qi: (bi, h, qi)),
        out_shape=jax.ShapeDtypeStruct((b, GROUP_WIDTH, s), BF16),
        scratch_shapes=_sweep_scratch(tk, tq) + [pltpu.VMEM((8, LANES), F32)],
        compiler_params=_cparams("arbitrary", "arbitrary", "arbitrary"),
        name="mla_attention",
    )(qt, k, vt)


def _dil_multiplicity(tq, tk):
    n_win = (tq + 2 * DIL_HALO) // tk
    key = jnp.arange(n_win * tk)[:, None] - DIL_HALO
    delta = key - jnp.arange(tq)[None, :]
    mult = jnp.zeros(delta.shape, F32)
    for window, dil in DIL_PATTERNS:
        reach = (window // 2 // dil) * dil
        mult = mult + ((delta % dil == 0) & (jnp.abs(delta) <= reach)).astype(F32)
    return mult.reshape(n_win, tk, tq)


def _dil_kernel(c_ref, qt_ref, k_ref, vt_ref, o_ref, s_ref, p_ref, *, n_seq_chunks):
    tq = qt_ref.shape[1]
    n_win, tk = c_ref.shape[0], c_ref.shape[1]
    halo_chunks = DIL_HALO // tk
    qi = pl.program_id(2)

    qt = qt_ref[...].astype(F32)
    row = lax.broadcasted_iota(jnp.int32, (LANES, tq), 0)
    w = jnp.concatenate([jnp.where(row < HEAD_DIM, qt, 0.0),
                         jnp.where(row >= HEAD_DIM, qt, 0.0)], axis=1).astype(BF16)

    def seq_chunk(c):
        return qi * (tq // tk) + c - halo_chunks

    def clamped(c):
        return jnp.clip(seq_chunk(c), 0, n_seq_chunks - 1)

    def mask_chunk(c):
        ck = seq_chunk(c)
        inside = jnp.where((ck >= 0) & (ck < n_seq_chunks), 1.0, 0.0)
        mult = c_ref[c] * inside
        return jnp.concatenate([mult, mult], axis=1)

    def k_chunk(c):
        return k_ref[pl.ds(pl.multiple_of(clamped(c) * tk, tk), tk), :]

    accs = _flash_sweep(
        n_win, k_chunk, w,
        [lambda c: vt_ref[0, clamped(c)], lambda c: vt_ref[1, clamped(c)]],
        [tq, tq], s_ref, p_ref, mask_chunk=mask_chunk)
    for hl, acc in enumerate(accs):
        o_ref[HEAD_DIM * hl:HEAD_DIM * (hl + 1), :] = (
            acc[0:HEAD_DIM] / acc[HEAD_DIM:HEAD_DIM + 1]).astype(BF16)


def _dil_attention(mult, qt, k, vt):
    b, _, s = qt.shape
    tq, tk = DIL_TQ, DIL_TK
    n_win = mult.shape[0]
    return pl.pallas_call(
        functools.partial(_dil_kernel, n_seq_chunks=s // tk),
        grid=(b, 2, s // tq),
        in_specs=[
            pl.BlockSpec((n_win, tk, tq), lambda bi, j, qi: (0, 0, 0)),
            pl.BlockSpec((None, LANES, tq), lambda bi, j, qi: (bi, j, qi)),
            pl.BlockSpec((None, s, LANES), lambda bi, j, qi: (bi, 0, j)),
            pl.BlockSpec((None, 2, s // tk, VT_ROWS, tk), lambda bi, j, qi: (bi, j, 0, 0, 0)),
        ],
        out_specs=pl.BlockSpec((None, LANES, tq), lambda bi, j, qi: (bi, j, qi)),
        out_shape=jax.ShapeDtypeStruct((b, GROUP_WIDTH, s), BF16),
        scratch_shapes=_sweep_scratch(tk, 2 * tq),
        compiler_params=_cparams("arbitrary", "arbitrary", "arbitrary"),
        name="dilated_attention",
    )(mult, qt, k, vt)


def _with_halo(prev_ref, cur_ref, next_ref, i, n_tiles):
    prev = jnp.where(i > 0, prev_ref[...], 0.0)
    nxt = jnp.where(i < n_tiles - 1, next_ref[...], 0.0)
    return jnp.concatenate([prev, cur_ref[...], nxt], axis=0)


def _shift_rows(x, k):
    n = x.shape[0]
    return pltpu.roll(x, (n - k) % n, 0)


def _pool_kernel(prev_ref, cur_ref, next_ref, w_ref, scale_ref, o_ref, *, seq_len):
    tm = cur_ref.shape[0]
    i = pl.program_id(1)
    x = _with_halo(prev_ref, cur_ref, next_ref, i, seq_len // tm)
    t = i * tm + lax.broadcasted_iota(jnp.int32, (tm, 1), 0)
    lane_group = lax.broadcasted_iota(jnp.int32, (tm, GROUP_WIDTH), 1) // POOL_GROUP

    run = x
    mean = None
    for g, wnd in enumerate(POOL_WINDOWS):
        run = run + _shift_rows(run, wnd // 2)
        total = _shift_rows(run, -(wnd // 2))[HALO:HALO + tm]
        lo = jnp.clip(t - wnd // 2, 0, seq_len - 1)
        hi = jnp.clip(t + wnd - wnd // 2 - 1, 0, seq_len - 1)
        cand = total / (hi - lo + 1).astype(F32)
        mean = cand if mean is None else jnp.where(lane_group == g, cand, mean)
    d = (mean - cur_ref[...]).astype(BF16)
    y = jnp.dot(d, w_ref[...], preferred_element_type=F32) * scale_ref[...]
    o_ref[...] = y.astype(BF16)


def _halo_specs(tm, cols, seq_len):
    nb = tm // HALO
    last = seq_len // HALO - 1
    return [
        pl.BlockSpec((None, HALO, cols), lambda bi, i: (bi, jnp.maximum(i * nb - 1, 0), 0)),
        pl.BlockSpec((None, tm, cols), lambda bi, i: (bi, i, 0)),
        pl.BlockSpec((None, HALO, cols), lambda bi, i: (bi, jnp.minimum((i + 1) * nb, last), 0)),
    ]


def _pool_mixer(hc3, w_bd, scale):
    b, s, c = hc3.shape
    tm = POOL_TM
    prev_s, cur_s, next_s = _halo_specs(tm, c, s)
    return pl.pallas_call(
        functools.partial(_pool_kernel, seq_len=s),
        grid=(b, s // tm),
        in_specs=[prev_s, cur_s, next_s,
                  pl.BlockSpec(w_bd.shape, lambda bi, i: (0, 0)),
                  pl.BlockSpec(scale.shape, lambda bi, i: (0, 0))],
        out_specs=pl.BlockSpec((None, tm, c), lambda bi, i: (bi, i, 0)),
        out_shape=jax.ShapeDtypeStruct((b, s, c), BF16),
        compiler_params=_cparams("arbitrary", "arbitrary"),
        name="pool_mixer",
    )(hc3, hc3, hc3, w_bd, scale)


def _layer_norm(z, g, b):
    mu = jnp.mean(z, axis=-1, keepdims=True)
    zc = z - mu
    var = jnp.mean(zc * zc, axis=-1, keepdims=True)
    return zc * lax.rsqrt(var + LN_EPS) * g + b


def _tdot(at, w):
    return lax.dot_general(at, w, (((0,), (0,)), ((), ())), preferred_element_type=F32)


def _out_kernel(x_ref, ya_ref, yb_ref, yc_ref, yd_ref, w_ref, g_ref, b_ref, o_ref):
    gw = GROUP_WIDTH
    z = _tdot(ya_ref[...], w_ref[0:gw, :])
    z = z + _tdot(yb_ref[...], w_ref[gw:2 * gw, :])
    z = z + jnp.dot(yc_ref[...], w_ref[2 * gw:3 * gw, :], preferred_element_type=F32)
    z = z + _tdot(yd_ref[...], w_ref[3 * gw:4 * gw, :])
    o_ref[...] = _layer_norm(DN_ALPHA * x_ref[...] + z, g_ref[...], b_ref[...])


def _out_proj(x3, ya_t, yb_t, yc, yd_t, w, g, bias):
    b, s, d = x3.shape
    tm = OUT_TM
    tok = lambda n: pl.BlockSpec((None, tm, n), lambda bi, i: (bi, i, 0))
    feat = pl.BlockSpec((None, GROUP_WIDTH, tm), lambda bi, i: (bi, 0, i))
    full = lambda a: pl.BlockSpec(a.shape, lambda bi, i: (0,) * a.ndim)
    return pl.pallas_call(
        _out_kernel,
        grid=(b, s // tm),
        in_specs=[tok(d), feat, feat, tok(GROUP_WIDTH), feat, full(w), full(g), full(bias)],
        out_specs=tok(d),
        out_shape=jax.ShapeDtypeStruct((b, s, d), F32),
        compiler_params=_cparams("arbitrary", "arbitrary"),
        name="out_proj_ln",
    )(x3, ya_t, yb_t, yc, yd_t, w, g, bias)


def _ffn_kernel(prev_ref, cur_ref, next_ref, wup_ref, wd_ref, cw_ref, g_ref, b_ref, o_ref,
                act_ref, *, seq_len):
    tm = cur_ref.shape[0]
    i = pl.program_id(1)
    xe = _with_halo(prev_ref, cur_ref, next_ref, i, seq_len // tm).astype(BF16)

    def conv_proj(j0):
        u = jnp.dot(xe, wup_ref[:, j0:j0 + FFN_CHUNK], preferred_element_type=F32)
        c = cw_ref[:, j0:j0 + FFN_CHUNK]
        y = _shift_rows(u, -1) * c[0:1] + u * c[1:2] + _shift_rows(u, 1) * c[2:3] + c[3:4]
        return y[HALO:HALO + tm]

    for j0 in range(0, D_FF, FFN_CHUNK):
        gate = conv_proj(j0)
        up = conv_proj(D_FF + j0)
        act_ref[:, j0:j0 + FFN_CHUNK] = (gate * jax.nn.sigmoid(gate) * up).astype(BF16)
    f = jnp.dot(act_ref[...], wd_ref[...], preferred_element_type=F32)
    o_ref[...] = _layer_norm(DN_ALPHA * cur_ref[...] + f, g_ref[...], b_ref[...])


def _ffn(x3, wup, wd, cw, g, bias):
    b, s, d = x3.shape
    tm = FFN_TM
    full = lambda a: pl.BlockSpec(a.shape, lambda bi, i: (0,) * a.ndim)
    return pl.pallas_call(
        functools.partial(_ffn_kernel, seq_len=s),
        grid=(b, s // tm),
        in_specs=_halo_specs(tm, d, s) + [full(wup), full(wd), full(cw), full(g), full(bias)],
        out_specs=pl.BlockSpec((None, tm, d), lambda bi, i: (bi, i, 0)),
        out_shape=jax.ShapeDtypeStruct((b, s, d), F32),
        scratch_shapes=[pltpu.VMEM((tm, D_FF), BF16)],
        compiler_params=_cparams("arbitrary", "arbitrary"),
        name="conv_ffn_ln",
    )(x3, x3, x3, wup, wd, cw, g, bias)


def _pad_cols(w, groups, width, padded):
    k = w.shape[0]
    w = w.reshape(k, groups, width)
    return jnp.pad(w, ((0, 0), (0, 0), (0, padded - width))).reshape(k, groups * padded)


def kernel(x, positions, w_in, diff_lambda, diff_subln, pool_w, pool_scale, mla_q_norm,
           mla_kv_norm, mla_w_uq, mla_w_ukv, w_out, ln1_g, ln1_b, ffn_w_up, ffn_conv_w,
           ffn_conv_b, ffn_w_down, ln2_g, ln2_b):
    b, s, d = x.shape
    t = b * s
    assert d == D_MODEL and s % 1024 == 0

    consts = jnp.stack([
        _rope_consts(DIFF_QK_DIM, DIFF_QK_DIM // 4, 0),
        _rope_consts(HEAD_DIM, HEAD_DIM // 4, 0),
        _rope_consts(LANES, MLA_ROPE_DIM, MLA_NOPE_DIM),
    ])
    tables = _rope_tables(positions.reshape(t, 1), consts)
    ta, tb, td = tables[0], tables[1], tables[2]
    mult = _dil_multiplicity(DIL_TQ, DIL_TK)

    for l in range(DEPTH):
        wi = w_in[l]
        w_proj = jnp.concatenate([
            wi[:, :7 * GROUP_WIDTH + MLA_Q_RANK + MLA_KV_RANK],
            jnp.zeros((d, MLA_NOPE_DIM), wi.dtype),
            wi[:, 7 * GROUP_WIDTH + MLA_Q_RANK + MLA_KV_RANK:],
            jnp.zeros((d, LANES - MLA_NOPE_DIM - MLA_ROPE_DIM), wi.dtype)], axis=1).astype(BF16)
        wq = _pad_cols(mla_w_uq[l], 4, MLA_NOPE_DIM + MLA_ROPE_DIM, LANES).astype(BF16)
        wkv = mla_w_ukv[l].reshape(MLA_KV_RANK, 4, 2, HEAD_DIM)
        wk = _pad_cols(wkv[:, :, 0].reshape(MLA_KV_RANK, 4 * HEAD_DIM), 4, HEAD_DIM, LANES).astype(BF16)
        wv = wkv[:, :, 1].reshape(MLA_KV_RANK, 4 * HEAD_DIM).astype(BF16)
        pw = pool_w[l]
        w_pool = jnp.zeros((GROUP_WIDTH, GROUP_WIDTH), pw.dtype)
        for g in range(4):
            w_pool = w_pool.at[g * POOL_GROUP:(g + 1) * POOL_GROUP,
                               g * POOL_GROUP:(g + 1) * POOL_GROUP].set(pw[g])
        w_pool = w_pool.astype(BF16)
        conv = jnp.concatenate([ffn_conv_w[l], ffn_conv_b[l][None, :],
                                jnp.zeros((4, 2 * D_FF), F32)], axis=0)

        qa_t, ka, va_t, qb_t, kb, vb_t, hc, hd = _proj(x, w_proj, ta, tb)
        qd_t, kd, vd_t = _mla_prep(hd, td, mla_q_norm[l][None, :], mla_kv_norm[l][None, :],
                                   wq, wk, wv)
        ya_t = _diff_attention(diff_lambda[l], diff_subln[l][:, None], qa_t, ka, va_t, l)
        yb_t = _dil_attention(mult, qb_t, kb, vb_t)
        yc = _pool_mixer(hc, w_pool, pool_scale[l][None, :])
        yd_t = _mla_attention(qd_t, kd, vd_t)

        x = _out_proj(x, ya_t, yb_t, yc, yd_t, w_out[l].astype(BF16),
                      ln1_g[l][None, :], ln1_b[l][None, :])
        x = _ffn(x, ffn_w_up[l].astype(BF16), ffn_w_down[l].astype(BF16), conv,
                 ln2_g[l][None, :], ln2_b[l][None, :])
    return x
```

```python
import functools
import math

import jax
import jax.numpy as jnp
from jax import lax
from jax.experimental import pallas as pl
from jax.experimental.pallas import tpu as pltpu

F32 = jnp.float32
BF16 = jnp.bfloat16

D_MODEL = 1024
DEPTH = 4
HEAD_DIM = 64
GROUP_WIDTH = 256
ROPE_THETA = 500000.0
NEG_INF = -1e30
LOG2E = math.log2(math.e)

DIFF_QK_DIM = 32
DIL_PATTERNS = ((128, 1), (512, 4), (2048, 16))
POOL_WINDOWS = (2, 4, 8, 16)
POOL_GROUP = 64
MLA_Q_RANK = 256
MLA_KV_RANK = 128
MLA_NOPE_DIM = 64
MLA_ROPE_DIM = 32
D_FF = 2816
DN_ALPHA = (2 * DEPTH) ** 0.25
LN_EPS = 1e-5
RMS_EPS = 1e-6

LANES = 128
SUM_ROWS = 16
VT_ROWS = HEAD_DIM + SUM_ROWS
DIL_HALO = 1024
VMEM_LIMIT = 56 * 1024 * 1024

PROJ_TM = 512
ATT_TK = 512
DIFF_TQ = 256
MLA_TQ = 512
DIL_TQ = 256
DIL_TK = 256
POOL_TM = 512
OUT_TM = 512
FFN_TM = 512
FFN_CHUNK = 256
HALO = 8


def _cparams(*sem):
    return pltpu.CompilerParams(dimension_semantics=sem, vmem_limit_bytes=VMEM_LIMIT)


def _rope_tables_kernel(pos_ref, c_ref, o_ref):
    pos = pos_ref[...].astype(F32)
    ang = pos * c_ref[0:1, :]
    cs = jnp.cos(ang)
    sn = jnp.sin(ang)
    is_rope = c_ref[1:2, :]
    o_ref[0] = cs * is_rope + (1.0 - is_rope)
    o_ref[1] = -sn * c_ref[2:3, :]
    o_ref[2] = sn * c_ref[3:4, :]


def _rope_consts(period, rot_dim, lane0):
    half = rot_dim // 2
    inv_freq = ROPE_THETA ** (-jnp.arange(half, dtype=F32) * 2.0 / rot_dim)
    lane = jnp.arange(LANES)
    d = lane % period - lane0
    is_rope = (d >= 0) & (d < rot_dim)
    f_idx = jnp.clip(d, 0, rot_dim - 1) % half
    rows = [
        jnp.where(is_rope, inv_freq[f_idx], 0.0),
        is_rope.astype(F32),
        (is_rope & (d < half)).astype(F32),
        (is_rope & (d >= half)).astype(F32),
    ]
    rows += [jnp.zeros((LANES,), F32)] * 4
    return jnp.stack(rows).astype(F32)


def _rope_tables(pos2d, consts):
    t = pos2d.shape[0]
    tm = 1024
    n = consts.shape[0]
    return pl.pallas_call(
        _rope_tables_kernel,
        grid=(n, t // tm),
        in_specs=[
            pl.BlockSpec((tm, 1), lambda j, i: (i, 0)),
            pl.BlockSpec((None, 8, LANES), lambda j, i: (j, 0, 0)),
        ],
        out_specs=pl.BlockSpec((None, 3, tm, LANES), lambda j, i: (j, 0, i, 0)),
        out_shape=jax.ShapeDtypeStruct((n, 3, t, LANES), F32),
        compiler_params=_cparams("arbitrary", "arbitrary"),
        name="rope_tables",
    )(pos2d, consts)


def _rope_apply(h, t_ref, half):
    cos, sa, sb = t_ref[0], t_ref[1], t_ref[2]
    outs = []
    for j in range(h.shape[1] // LANES):
        xj = h[:, LANES * j:LANES * (j + 1)]
        fwd = pltpu.roll(xj, LANES - half, 1)
        bwd = pltpu.roll(xj, half, 1)
        outs.append(xj * cos + fwd * sa + bwd * sb)
    return outs[0] if len(outs) == 1 else jnp.concatenate(outs, axis=1)


PROJ_COLS = 6 * GROUP_WIDTH + GROUP_WIDTH + 512


def _proj_kernel(x_ref, w_ref, ta_ref, tb_ref,
                 qa_ref, ka_ref, va_ref, qb_ref, kb_ref, vb_ref, hc_ref, hd_ref):
    x = x_ref[...].astype(BF16)

    def seg(j0, n):
        return jnp.dot(x, w_ref[:, j0:j0 + n], preferred_element_type=F32)

    g = GROUP_WIDTH
    qa = _rope_apply(seg(0, g), ta_ref, 4) * (LOG2E * DIFF_QK_DIM ** -0.5)
    qa_ref[...] = qa.T.astype(BF16)
    ka_ref[...] = _rope_apply(seg(g, g), ta_ref, 4).astype(BF16)
    _store_values_t(seg(2 * g, g), va_ref)
    qb = _rope_apply(seg(3 * g, g), tb_ref, 8) * (LOG2E * HEAD_DIM ** -0.5)
    qb_ref[...] = qb.T.astype(BF16)
    kb_ref[...] = _rope_apply(seg(4 * g, g), tb_ref, 8).astype(BF16)
    _store_values_t(seg(5 * g, g), vb_ref)
    hc_ref[...] = seg(6 * g, g)
    hd_ref[...] = seg(7 * g, 512)


def _store_values_t(v, vt_ref):
    tk = vt_ref.shape[3]
    vt = v.astype(BF16).T
    ones = jnp.ones((SUM_ROWS, tk), BF16)
    for h in range(4):
        for c in range(vt_ref.shape[1]):
            vt_ref[h, c, 0:HEAD_DIM, :] = vt[HEAD_DIM * h:HEAD_DIM * (h + 1), tk * c:tk * (c + 1)]
            vt_ref[h, c, HEAD_DIM:VT_ROWS, :] = ones


def _token_spec(tm, n):
    return pl.BlockSpec((None, tm, n), lambda bi, i: (bi, i, 0))


def _feature_spec(tm, n):
    return pl.BlockSpec((None, n, tm), lambda bi, i: (bi, 0, i))


def _values_spec(tm, tk):
    return pl.BlockSpec((None, 4, tm // tk, VT_ROWS, tk), lambda bi, i: (bi, 0, i, 0, 0))


def _values_shape(b, s, tk):
    return jax.ShapeDtypeStruct((b, 4, s // tk, VT_ROWS, tk), BF16)


def _table_spec(tm, s):
    return pl.BlockSpec((3, tm, LANES), lambda bi, i: (0, bi * (s // tm) + i, 0))


def _proj(x3, w, ta, tb):
    b, s, d = x3.shape
    tm = PROJ_TM
    g = GROUP_WIDTH
    tok = lambda n, dt: jax.ShapeDtypeStruct((b, s, n), dt)
    feat = jax.ShapeDtypeStruct((b, g, s), BF16)
    return pl.pallas_call(
        _proj_kernel,
        grid=(b, s // tm),
        in_specs=[_token_spec(tm, d), pl.BlockSpec((d, PROJ_COLS), lambda bi, i: (0, 0)),
                  _table_spec(tm, s), _table_spec(tm, s)],
        out_specs=[_feature_spec(tm, g), _token_spec(tm, g), _values_spec(tm, ATT_TK),
                   _feature_spec(tm, g), _token_spec(tm, g), _values_spec(tm, DIL_TK),
                   _token_spec(tm, g), _token_spec(tm, 512)],
        out_shape=[feat, tok(g, BF16), _values_shape(b, s, ATT_TK),
                   feat, tok(g, BF16), _values_shape(b, s, DIL_TK),
                   tok(g, F32), tok(512, F32)],
        compiler_params=_cparams("arbitrary", "arbitrary"),
        name="proj_rope",
    )(x3, w, ta, tb)


def _rms(x, g):
    return x * lax.rsqrt(jnp.mean(x * x, axis=-1, keepdims=True) + RMS_EPS) * g


def _mla_prep_kernel(hd_ref, td_ref, gq_ref, gkv_ref, wq_ref, wk_ref, wv_ref,
                     q_ref, k_ref, v_ref):
    hd = hd_ref[...]
    cq = _rms(hd[:, 0:MLA_Q_RANK], gq_ref[...]).astype(BF16)
    ckv = _rms(hd[:, MLA_Q_RANK:MLA_Q_RANK + MLA_KV_RANK], gkv_ref[...]).astype(BF16)
    q = jnp.dot(cq, wq_ref[...], preferred_element_type=F32)
    scale = LOG2E * (MLA_NOPE_DIM + MLA_ROPE_DIM) ** -0.5
    q_ref[...] = (_rope_apply(q, td_ref, MLA_ROPE_DIM // 2) * scale).T.astype(BF16)
    k_rope = _rope_apply(hd[:, 384:512], td_ref, MLA_ROPE_DIM // 2)
    k = jnp.dot(ckv, wk_ref[...], preferred_element_type=F32)
    k_ref[...] = (k + jnp.concatenate([k_rope] * 4, axis=1)).astype(BF16)
    _store_values_t(jnp.dot(ckv, wv_ref[...], preferred_element_type=F32), v_ref)


def _mla_prep(hd, td, gq, gkv, wq, wk, wv):
    b, s, _ = hd.shape
    tm = PROJ_TM
    full = lambda a: pl.BlockSpec(a.shape, lambda bi, i: (0,) * a.ndim)
    return pl.pallas_call(
        _mla_prep_kernel,
        grid=(b, s // tm),
        in_specs=[_token_spec(tm, 512), _table_spec(tm, s),
                  full(gq), full(gkv), full(wq), full(wk), full(wv)],
        out_specs=[_feature_spec(tm, 512), _token_spec(tm, 512), _values_spec(tm, ATT_TK)],
        out_shape=[jax.ShapeDtypeStruct((b, 512, s), BF16),
                   jax.ShapeDtypeStruct((b, s, 512), BF16), _values_shape(b, s, ATT_TK)],
        compiler_params=_cparams("arbitrary", "arbitrary"),
        name="mla_prep",
    )(hd, td, gq, gkv, wq, wk, wv)


def _flash_sweep(n_chunks, k_chunk, w, v_chunks, widths, s_ref, p_ref, mask_chunk=None):
    n = w.shape[1]
    offs = [sum(widths[:g]) for g in range(len(widths))]
    last = n_chunks - 1

    def scores(c, slot):
        s = jnp.dot(k_chunk(c), w, preferred_element_type=F32)
        if mask_chunk is not None:
            s = jnp.where(mask_chunk(c) > 0.0, s, NEG_INF)
        s_ref[slot] = s
        return jnp.max(s, axis=0, keepdims=True)

    def values(c, slot, accs):
        return tuple(
            acc + jnp.dot(v_chunks[g](c), p_ref[slot, :, offs[g]:offs[g] + widths[g]],
                          preferred_element_type=F32)
            for g, acc in enumerate(accs))

    def step(c, slot, carry):
        m, cmaxes, accs = carry
        m_new = jnp.maximum(m, cmaxes[0])
        alpha = jnp.exp2(m - m_new)
        cmax_new = scores(jnp.minimum(c + SWEEP_AHEAD, last), (slot + SWEEP_AHEAD) % SWEEP_SLOTS)
        accs = values(jnp.maximum(c - 1, 0), (slot + 1) % 2, accs)
        accs = tuple(acc * alpha[:, offs[g]:offs[g] + widths[g]] for g, acc in enumerate(accs))
        p = jnp.exp2(s_ref[slot] - m_new)
        if mask_chunk is not None:
            p = p * mask_chunk(c)
        p_ref[slot % 2] = p.astype(BF16)
        return m_new, cmaxes[1:] + (cmax_new,), accs

    p_ref[1] = jnp.zeros(p_ref.shape[1:], BF16)
    carry = (jnp.full((1, n), NEG_INF, F32),
             tuple(scores(min(i, last), i) for i in range(SWEEP_AHEAD)),
             tuple(jnp.zeros((VT_ROWS, wd), F32) for wd in widths))

    def trip(j, carry):
        for i in range(SWEEP_STEPS):
            carry = step(SWEEP_STEPS * j + i, i % SWEEP_SLOTS, carry)
        return carry

    looped = n_chunks // SWEEP_STEPS * SWEEP_STEPS
    carry = lax.fori_loop(0, n_chunks // SWEEP_STEPS, trip, carry)
    for c in range(looped, n_chunks):
        carry = step(c, c % SWEEP_SLOTS, carry)
    return values(last, last % 2, carry[2])


SWEEP_SLOTS = 4
SWEEP_AHEAD = 2
SWEEP_STEPS = 8


def _sweep_scratch(tk, n):
    return [pltpu.VMEM((SWEEP_SLOTS, tk, n), F32), pltpu.VMEM((2, tk, n), BF16)]


SAFE_LOG2_SPAN = 100.0
KMAX_ROWS = 1024


def _key_absmax(k_ref, kmax_ref, refresh):
    @pl.when(refresh)
    def _():
        def body(i, m):
            blk = k_ref[pl.ds(pl.multiple_of(i * KMAX_ROWS, KMAX_ROWS), KMAX_ROWS), :]
            return jnp.maximum(m, jnp.max(jnp.abs(blk.astype(F32)), axis=0, keepdims=True))
        m = lax.fori_loop(0, k_ref.shape[0] // KMAX_ROWS, body, jnp.zeros((1, LANES), F32))
        kmax_ref[...] = jnp.broadcast_to(m, kmax_ref.shape)


def _fixed_sweep(n_chunks, k_chunk, w, v_chunks, widths, s_ref, p_ref, m_ref, mask_chunk, first):
    n = w.shape[1]
    offs = [sum(widths[:g]) for g in range(len(widths))]

    def chunk_of(t):
        if isinstance(t, int):
            return first if t == 0 else (t if t > first else t - 1)
        if first == 0:
            return t
        return jnp.where(t == 0, first, jnp.where(t > first, t, t - 1))

    def probs(s, t, slot, lsum):
        p = jnp.exp2(s - m_ref)
        if mask_chunk is not None:
            p = p * mask_chunk(chunk_of(t))
        p_ref[slot] = p.astype(BF16)
        return lsum + jnp.sum(p.reshape(-1, 8, n), axis=0)

    def values(t, slot, accs):
        c = chunk_of(t)
        return tuple(
            acc + jnp.dot(v_chunks[g](c)[0:HEAD_DIM], p_ref[slot, :, offs[g]:offs[g] + widths[g]],
                          preferred_element_type=F32)
            for g, acc in enumerate(accs))

    def step(t, slot, carry):
        lsum, accs = carry
        s = jnp.dot(k_chunk(chunk_of(t)), w, preferred_element_type=F32)
        accs = values(t - 1, 1 - slot, accs)
        return probs(s, t, slot, lsum), accs

    carry = (probs(s_ref[0], 0, 0, jnp.zeros((8, n), F32)),
             tuple(jnp.zeros((HEAD_DIM, wd), F32) for wd in widths))

    def trip(j, carry):
        for i in range(SWEEP_STEPS):
            carry = step(1 + SWEEP_STEPS * j + i, (1 + i) % 2, carry)
        return carry

    trips = (n_chunks - 1) // SWEEP_STEPS
    carry = lax.fori_loop(0, trips, trip, carry)
    for t in range(1 + trips * SWEEP_STEPS, n_chunks):
        carry = step(t, t % 2, carry)
    lsum, accs = carry
    accs = values(n_chunks - 1, (n_chunks - 1) % 2, accs)
    denom = jnp.broadcast_to(jnp.sum(lsum, axis=0, keepdims=True), (SUM_ROWS, n))
    return tuple(jnp.concatenate([acc, denom[:, offs[g]:offs[g] + widths[g]]], axis=0)
                 for g, acc in enumerate(accs))


def _softmax_sweep(n_chunks, k_chunk, w, v_chunks, widths, s_ref, p_ref, kmax_ref,
                   mask_chunk=None, first=0):
    s0 = jnp.dot(k_chunk(first), w, preferred_element_type=F32)
    s_ref[0] = s0
    if mask_chunk is not None:
        s0 = jnp.where(mask_chunk(first) > 0.0, s0, NEG_INF)
    m0 = jnp.max(s0, axis=0, keepdims=True)
    kmax = (kmax_ref[...] * 1.01).astype(BF16)
    bound = jnp.dot(kmax, jnp.abs(w), preferred_element_type=F32)[0:1]
    safe = jnp.max(bound - m0) <= SAFE_LOG2_SPAN
    return lax.cond(
        safe,
        lambda: _fixed_sweep(n_chunks, k_chunk, w, v_chunks, widths, s_ref, p_ref, m0,
                             mask_chunk, first),
        lambda: _flash_sweep(n_chunks, k_chunk, w, v_chunks, widths, s_ref, p_ref, mask_chunk))


def _diff_kernel(lam_ref, g_ref, qt_ref, k_ref, vt_ref, o_ref, s_ref, p_ref, kmax_ref, *,
                 lam_init, n_chunks):
    tq = qt_ref.shape[1]
    tk = vt_ref.shape[3]
    lp = lam_ref[...]
    lam = (jnp.exp(jnp.sum(lp[0:1] * lp[1:2], axis=1, keepdims=True))
           - jnp.exp(jnp.sum(lp[2:3] * lp[3:4], axis=1, keepdims=True)) + lam_init)

    qt = qt_ref[...].astype(F32)
    row = lax.broadcasted_iota(jnp.int32, (LANES, tq), 0)
    cols = [jnp.where((row >= DIFF_QK_DIM * c) & (row < DIFF_QK_DIM * (c + 1)), qt, 0.0)
            for c in range(4)]
    w = jnp.concatenate(cols, axis=1).astype(BF16)

    _key_absmax(k_ref, kmax_ref, pl.program_id(2) == 0)
    accs = _softmax_sweep(
        n_chunks,
        lambda c: k_ref[pl.ds(pl.multiple_of(c * tk, tk), tk), :],
        w,
        [lambda c: vt_ref[0, c], lambda c: vt_ref[1, c]],
        [2 * tq, 2 * tq], s_ref, p_ref, kmax_ref)

    gain = g_ref[...] * (1.0 - lam_init)
    for hl, acc in enumerate(accs):
        o1 = acc[0:HEAD_DIM, 0:tq] / acc[HEAD_DIM:HEAD_DIM + 1, 0:tq]
        o2 = acc[0:HEAD_DIM, tq:2 * tq] / acc[HEAD_DIM:HEAD_DIM + 1, tq:2 * tq]
        o = o1 - lam * o2
        ms = jnp.mean(o * o, axis=0, keepdims=True)
        y = o * lax.rsqrt(ms + RMS_EPS) * gain
        o_ref[HEAD_DIM * hl:HEAD_DIM * (hl + 1), :] = y.astype(BF16)


def _diff_attention(lam_params, subln_g, qt, k, vt, layer_idx):
    b, _, s = qt.shape
    tq, tk = DIFF_TQ, ATT_TK
    nck = s // tk
    lam_init = 0.8 - 0.6 * math.exp(-0.3 * layer_idx)
    kern = functools.partial(_diff_kernel, lam_init=lam_init, n_chunks=nck)
    return pl.pallas_call(
        kern,
        grid=(b, 2, s // tq),
        in_specs=[
            pl.BlockSpec((4, DIFF_QK_DIM), lambda bi, j, qi: (0, 0)),
            pl.BlockSpec((HEAD_DIM, 1), lambda bi, j, qi: (0, 0)),
            pl.BlockSpec((None, LANES, tq), lambda bi, j, qi: (bi, j, qi)),
            pl.BlockSpec((None, s, LANES), lambda bi, j, qi: (bi, 0, j)),
            pl.BlockSpec((None, 2, nck, VT_ROWS, tk), lambda bi, j, qi: (bi, j, 0, 0, 0)),
        ],
        out_specs=pl.BlockSpec((None, LANES, tq), lambda bi, j, qi: (bi, j, qi)),
        out_shape=jax.ShapeDtypeStruct((b, GROUP_WIDTH, s), BF16),
        scratch_shapes=_sweep_scratch(tk, 4 * tq) + [pltpu.VMEM((8, LANES), F32)],
        compiler_params=_cparams("arbitrary", "arbitrary", "arbitrary"),
        name="diff_attention",
    )(lam_params, subln_g, qt, k, vt)


def _mla_kernel(qt_ref, k_ref, vt_ref, o_ref, s_ref, p_ref, kmax_ref, *, n_chunks):
    tq = qt_ref.shape[1]
    tk = vt_ref.shape[2]
    _key_absmax(k_ref, kmax_ref, pl.program_id(2) == 0)
    (acc,) = _softmax_sweep(
        n_chunks,
        lambda c: k_ref[pl.ds(pl.multiple_of(c * tk, tk), tk), :],
        qt_ref[...],
        [lambda c: vt_ref[c]],
        [tq], s_ref, p_ref, kmax_ref)
    o_ref[...] = (acc[0:HEAD_DIM] / acc[HEAD_DIM:HEAD_DIM + 1]).astype(BF16)


def _mla_attention(qt, k, vt):
    b, _, s = qt.shape
    tq, tk = MLA_TQ, ATT_TK
    nck = s // tk
    return pl.pallas_call(
        functools.partial(_mla_kernel, n_chunks=nck),
        grid=(b, 4, s // tq),
        in_specs=[
            pl.BlockSpec((None, LANES, tq), lambda bi, h, qi: (bi, h, qi)),
            pl.BlockSpec((None, s, LANES), lambda bi, h, qi: (bi, 0, h)),
            pl.BlockSpec((None, None, nck, VT_ROWS, tk), lambda bi, h, qi: (bi, h, 0, 0, 0)),
        ],
        out_specs=pl.BlockSpec((None, HEAD_DIM, tq), lambda bi, h, qi: (bi, h, qi)),
        out_shape=jax.ShapeDtypeStruct((b, GROUP_WIDTH, s), BF16),
        scratch_shapes=_sweep_scratch(tk, tq) + [pltpu.VMEM((8, LANES), F32)],
        compiler_params=_cparams("arbitrary", "arbitrary", "arbitrary"),
        name="mla_attention",
    )(qt, k, vt)


def _dil_multiplicity(tq, tk):
    n_win = (tq + 2 * DIL_HALO) // tk
    key = jnp.arange(n_win * tk)[:, None] - DIL_HALO
    delta = key - jnp.arange(tq)[None, :]
    mult = jnp.zeros(delta.shape, F32)
    for window, dil in DIL_PATTERNS:
        reach = (window // 2 // dil) * dil
        mult = mult + ((delta % dil == 0) & (jnp.abs(delta) <= reach)).astype(F32)
    return mult.reshape(n_win, tk, tq)


def _dil_kernel(c_ref, qt_ref, k_ref, vt_ref, o_ref, s_ref, p_ref, kmax_ref, *, n_seq_chunks):
    tq = qt_ref.shape[1]
    n_win, tk = c_ref.shape[0], c_ref.shape[1]
    halo_chunks = DIL_HALO // tk
    qi = pl.program_id(2)

    qt = qt_ref[...].astype(F32)
    row = lax.broadcasted_iota(jnp.int32, (LANES, tq), 0)
    w = jnp.concatenate([jnp.where(row < HEAD_DIM, qt, 0.0),
                         jnp.where(row >= HEAD_DIM, qt, 0.0)], axis=1).astype(BF16)

    def seq_chunk(c):
        return qi * (tq // tk) + c - halo_chunks

    def clamped(c):
        return jnp.clip(seq_chunk(c), 0, n_seq_chunks - 1)

    def mask_chunk(c):
        ck = seq_chunk(c)
        inside = jnp.where((ck >= 0) & (ck < n_seq_chunks), 1.0, 0.0)
        mult = c_ref[c] * inside
        return jnp.concatenate([mult, mult], axis=1)

    def k_chunk(c):
        return k_ref[pl.ds(pl.multiple_of(clamped(c) * tk, tk), tk), :]

    _key_absmax(k_ref, kmax_ref, qi == 0)
    accs = _softmax_sweep(
        n_win, k_chunk, w,
        [lambda c: vt_ref[0, clamped(c)], lambda c: vt_ref[1, clamped(c)]],
        [tq, tq], s_ref, p_ref, kmax_ref, mask_chunk=mask_chunk, first=n_win // 2)
    for hl, acc in enumerate(accs):
        o_ref[HEAD_DIM * hl:HEAD_DIM * (hl + 1), :] = (
            acc[0:HEAD_DIM] / acc[HEAD_DIM:HEAD_DIM + 1]).astype(BF16)


def _dil_attention(mult, qt, k, vt):
    b, _, s = qt.shape
    tq, tk = DIL_TQ, DIL_TK
    n_win = mult.shape[0]
    return pl.pallas_call(
        functools.partial(_dil_kernel, n_seq_chunks=s // tk),
        grid=(b, 2, s // tq),
        in_specs=[
            pl.BlockSpec((n_win, tk, tq), lambda bi, j, qi: (0, 0, 0)),
            pl.BlockSpec((None, LANES, tq), lambda bi, j, qi: (bi, j, qi)),
            pl.BlockSpec((None, s, LANES), lambda bi, j, qi: (bi, 0, j)),
            pl.BlockSpec((None, 2, s // tk, VT_ROWS, tk), lambda bi, j, qi: (bi, j, 0, 0, 0)),
        ],
        out_specs=pl.BlockSpec((None, LANES, tq), lambda bi, j, qi: (bi, j, qi)),
        out_shape=jax.ShapeDtypeStruct((b, GROUP_WIDTH, s), BF16),
        scratch_shapes=_sweep_scratch(tk, 2 * tq) + [pltpu.VMEM((8, LANES), F32)],
        compiler_params=_cparams("arbitrary", "arbitrary", "arbitrary"),
        name="dilated_attention",
    )(mult, qt, k, vt)


def _with_halo(prev_ref, cur_ref, next_ref, i, n_tiles):
    prev = jnp.where(i > 0, prev_ref[...], 0.0)
    nxt = jnp.where(i < n_tiles - 1, next_ref[...], 0.0)
    return jnp.concatenate([prev, cur_ref[...], nxt], axis=0)


def _shift_rows(x, k):
    n = x.shape[0]
    return pltpu.roll(x, (n - k) % n, 0)


def _pool_kernel(prev_ref, cur_ref, next_ref, w_ref, scale_ref, o_ref, *, seq_len):
    tm = cur_ref.shape[0]
    i = pl.program_id(1)
    x = _with_halo(prev_ref, cur_ref, next_ref, i, seq_len // tm)
    t = i * tm + lax.broadcasted_iota(jnp.int32, (tm, 1), 0)
    lane_group = lax.broadcasted_iota(jnp.int32, (tm, GROUP_WIDTH), 1) // POOL_GROUP

    run = x
    mean = None
    for g, wnd in enumerate(POOL_WINDOWS):
        run = run + _shift_rows(run, wnd // 2)
        total = _shift_rows(run, -(wnd // 2))[HALO:HALO + tm]
        lo = jnp.clip(t - wnd // 2, 0, seq_len - 1)
        hi = jnp.clip(t + wnd - wnd // 2 - 1, 0, seq_len - 1)
        cand = total / (hi - lo + 1).astype(F32)
        mean = cand if mean is None else jnp.where(lane_group == g, cand, mean)
    d = (mean - cur_ref[...]).astype(BF16)
    y = jnp.dot(d, w_ref[...], preferred_element_type=F32) * scale_ref[...]
    o_ref[...] = y.astype(BF16)


def _halo_specs(tm, cols, seq_len):
    nb = tm // HALO
    last = seq_len // HALO - 1
    return [
        pl.BlockSpec((None, HALO, cols), lambda bi, i: (bi, jnp.maximum(i * nb - 1, 0), 0)),
        pl.BlockSpec((None, tm, cols), lambda bi, i: (bi, i, 0)),
        pl.BlockSpec((None, HALO, cols), lambda bi, i: (bi, jnp.minimum((i + 1) * nb, last), 0)),
    ]


def _pool_mixer(hc3, w_bd, scale):
    b, s, c = hc3.shape
    tm = POOL_TM
    prev_s, cur_s, next_s = _halo_specs(tm, c, s)
    return pl.pallas_call(
        functools.partial(_pool_kernel, seq_len=s),
        grid=(b, s // tm),
        in_specs=[prev_s, cur_s, next_s,
                  pl.BlockSpec(w_bd.shape, lambda bi, i: (0, 0)),
                  pl.BlockSpec(scale.shape, lambda bi, i: (0, 0))],
        out_specs=pl.BlockSpec((None, tm, c), lambda bi, i: (bi, i, 0)),
        out_shape=jax.ShapeDtypeStruct((b, s, c), BF16),
        compiler_params=_cparams("arbitrary", "arbitrary"),
        name="pool_mixer",
    )(hc3, hc3, hc3, w_bd, scale)


def _layer_norm(z, g, b):
    mu = jnp.mean(z, axis=-1, keepdims=True)
    zc = z - mu
    var = jnp.mean(zc * zc, axis=-1, keepdims=True)
    return zc * lax.rsqrt(var + LN_EPS) * g + b


def _tdot(at, w):
    return lax.dot_general(at, w, (((0,), (0,)), ((), ())), preferred_element_type=F32)


def _out_kernel(x_ref, ya_ref, yb_ref, yc_ref, yd_ref, w_ref, g_ref, b_ref, o_ref):
    gw = GROUP_WIDTH
    z = _tdot(ya_ref[...], w_ref[0:gw, :])
    z = z + _tdot(yb_ref[...], w_ref[gw:2 * gw, :])
    z = z + jnp.dot(yc_ref[...], w_ref[2 * gw:3 * gw, :], preferred_element_type=F32)
    z = z + _tdot(yd_ref[...], w_ref[3 * gw:4 * gw, :])
    o_ref[...] = _layer_norm(DN_ALPHA * x_ref[...] + z, g_ref[...], b_ref[...])


def _out_proj(x3, ya_t, yb_t, yc, yd_t, w, g, bias):
    b, s, d = x3.shape
    tm = OUT_TM
    tok = lambda n: pl.BlockSpec((None, tm, n), lambda bi, i: (bi, i, 0))
    feat = pl.BlockSpec((None, GROUP_WIDTH, tm), lambda bi, i: (bi, 0, i))
    full = lambda a: pl.BlockSpec(a.shape, lambda bi, i: (0,) * a.ndim)
    return pl.pallas_call(
        _out_kernel,
        grid=(b, s // tm),
        in_specs=[tok(d), feat, feat, tok(GROUP_WIDTH), feat, full(w), full(g), full(bias)],
        out_specs=tok(d),
        out_shape=jax.ShapeDtypeStruct((b, s, d), F32),
        compiler_params=_cparams("arbitrary", "arbitrary"),
        name="out_proj_ln",
    )(x3, ya_t, yb_t, yc, yd_t, w, g, bias)


def _ffn_kernel(prev_ref, cur_ref, next_ref, wup_ref, wd_ref, cw_ref, g_ref, b_ref, o_ref,
                act_ref, *, seq_len):
    tm = cur_ref.shape[0]
    i = pl.program_id(1)
    xe = _with_halo(prev_ref, cur_ref, next_ref, i, seq_len // tm).astype(BF16)

    def conv_proj(j0):
        u = jnp.dot(xe, wup_ref[:, j0:j0 + FFN_CHUNK], preferred_element_type=F32)
        c = cw_ref[:, j0:j0 + FFN_CHUNK]
        y = _shift_rows(u, -1) * c[0:1] + u * c[1:2] + _shift_rows(u, 1) * c[2:3] + c[3:4]
        return y[HALO:HALO + tm]

    for j0 in range(0, D_FF, FFN_CHUNK):
        gate = conv_proj(j0)
        up = conv_proj(D_FF + j0)
        act_ref[:, j0:j0 + FFN_CHUNK] = (gate * jax.nn.sigmoid(gate) * up).astype(BF16)
    f = jnp.dot(act_ref[...], wd_ref[...], preferred_element_type=F32)
    o_ref[...] = _layer_norm(DN_ALPHA * cur_ref[...] + f, g_ref[...], b_ref[...])


def _ffn(x3, wup, wd, cw, g, bias):
    b, s, d = x3.shape
    tm = FFN_TM
    full = lambda a: pl.BlockSpec(a.shape, lambda bi, i: (0,) * a.ndim)
    return pl.pallas_call(
        functools.partial(_ffn_kernel, seq_len=s),
        grid=(b, s // tm),
        in_specs=_halo_specs(tm, d, s) + [full(wup), full(wd), full(cw), full(g), full(bias)],
        out_specs=pl.BlockSpec((None, tm, d), lambda bi, i: (bi, i, 0)),
        out_shape=jax.ShapeDtypeStruct((b, s, d), F32),
        scratch_shapes=[pltpu.VMEM((tm, D_FF), BF16)],
        compiler_params=_cparams("arbitrary", "arbitrary"),
        name="conv_ffn_ln",
    )(x3, x3, x3, wup, wd, cw, g, bias)


def _pad_cols(w, groups, width, padded):
    k = w.shape[0]
    w = w.reshape(k, groups, width)
    return jnp.pad(w, ((0, 0), (0, 0), (0, padded - width))).reshape(k, groups * padded)


def kernel(x, positions, w_in, diff_lambda, diff_subln, pool_w, pool_scale, mla_q_norm,
           mla_kv_norm, mla_w_uq, mla_w_ukv, w_out, ln1_g, ln1_b, ffn_w_up, ffn_conv_w,
           ffn_conv_b, ffn_w_down, ln2_g, ln2_b):
    b, s, d = x.shape
    t = b * s
    assert d == D_MODEL and s % 1024 == 0

    consts = jnp.stack([
        _rope_consts(DIFF_QK_DIM, DIFF_QK_DIM // 4, 0),
        _rope_consts(HEAD_DIM, HEAD_DIM // 4, 0),
        _rope_consts(LANES, MLA_ROPE_DIM, MLA_NOPE_DIM),
    ])
    tables = _rope_tables(positions.reshape(t, 1), consts)
    ta, tb, td = tables[0], tables[1], tables[2]
    mult = _dil_multiplicity(DIL_TQ, DIL_TK)

    for l in range(DEPTH):
        wi = w_in[l]
        w_proj = jnp.concatenate([
            wi[:, :7 * GROUP_WIDTH + MLA_Q_RANK + MLA_KV_RANK],
            jnp.zeros((d, MLA_NOPE_DIM), wi.dtype),
            wi[:, 7 * GROUP_WIDTH + MLA_Q_RANK + MLA_KV_RANK:],
            jnp.zeros((d, LANES - MLA_NOPE_DIM - MLA_ROPE_DIM), wi.dtype)], axis=1).astype(BF16)
        wq = _pad_cols(mla_w_uq[l], 4, MLA_NOPE_DIM + MLA_ROPE_DIM, LANES).astype(BF16)
        wkv = mla_w_ukv[l].reshape(MLA_KV_RANK, 4, 2, HEAD_DIM)
        wk = _pad_cols(wkv[:, :, 0].reshape(MLA_KV_RANK, 4 * HEAD_DIM), 4, HEAD_DIM, LANES).astype(BF16)
        wv = wkv[:, :, 1].reshape(MLA_KV_RANK, 4 * HEAD_DIM).astype(BF16)
        pw = pool_w[l]
        w_pool = jnp.zeros((GROUP_WIDTH, GROUP_WIDTH), pw.dtype)
        for g in range(4):
            w_pool = w_pool.at[g * POOL_GROUP:(g + 1) * POOL_GROUP,
                               g * POOL_GROUP:(g + 1) * POOL_GROUP].set(pw[g])
        w_pool = w_pool.astype(BF16)
        conv = jnp.concatenate([ffn_conv_w[l], ffn_conv_b[l][None, :],
                                jnp.zeros((4, 2 * D_FF), F32)], axis=0)

        qa_t, ka, va_t, qb_t, kb, vb_t, hc, hd = _proj(x, w_proj, ta, tb)
        qd_t, kd, vd_t = _mla_prep(hd, td, mla_q_norm[l][None, :], mla_kv_norm[l][None, :],
                                   wq, wk, wv)
        ya_t = _diff_attention(diff_lambda[l], diff_subln[l][:, None], qa_t, ka, va_t, l)
        yb_t = _dil_attention(mult, qb_t, kb, vb_t)
        yc = _pool_mixer(hc, w_pool, pool_scale[l][None, :])
        yd_t = _mla_attention(qd_t, kd, vd_t)

        x = _out_proj(x, ya_t, yb_t, yc, yd_t, w_out[l].astype(BF16),
                      ln1_g[l][None, :], ln1_b[l][None, :])
        x = _ffn(x, ffn_w_up[l].astype(BF16), ffn_w_down[l].astype(BF16), conv,
                 ln2_g[l][None, :], ln2_b[l][None, :])
    return x
```

```python
import functools
import math

import jax
import jax.numpy as jnp
from jax import lax
from jax.experimental import pallas as pl
from jax.experimental.pallas import tpu as pltpu

F32 = jnp.float32
BF16 = jnp.bfloat16

D_MODEL = 1024
DEPTH = 4
HEAD_DIM = 64
GROUP_WIDTH = 256
ROPE_THETA = 500000.0
NEG_INF = -1e30
LOG2E = math.log2(math.e)

DIFF_QK_DIM = 32
DIL_PATTERNS = ((128, 1), (512, 4), (2048, 16))
POOL_WINDOWS = (2, 4, 8, 16)
POOL_GROUP = 64
MLA_Q_RANK = 256
MLA_KV_RANK = 128
MLA_NOPE_DIM = 64
MLA_ROPE_DIM = 32
D_FF = 2816
DN_ALPHA = (2 * DEPTH) ** 0.25
LN_EPS = 1e-5
RMS_EPS = 1e-6

LANES = 128
SUM_ROWS = 16
VT_ROWS = HEAD_DIM + SUM_ROWS
DIL_HALO = 1024
VMEM_LIMIT = 56 * 1024 * 1024

PROJ_TM = 512
ATT_TK = 512
DIFF_TQ = 256
MLA_TQ = 512
DIL_TQ = 256
DIL_TK = 256
POOL_TM = 512
OUT_TM = 512
FFN_TM = 512
FFN_CHUNK = 256
HALO = 8


def _cparams(*sem):
    return pltpu.CompilerParams(dimension_semantics=sem, vmem_limit_bytes=VMEM_LIMIT)


def _rope_tables_kernel(pos_ref, c_ref, o_ref):
    pos = pos_ref[...].astype(F32)
    ang = pos * c_ref[0:1, :]
    cs = jnp.cos(ang)
    sn = jnp.sin(ang)
    is_rope = c_ref[1:2, :]
    o_ref[0] = cs * is_rope + (1.0 - is_rope)
    o_ref[1] = -sn * c_ref[2:3, :]
    o_ref[2] = sn * c_ref[3:4, :]


def _rope_consts(period, rot_dim, lane0):
    half = rot_dim // 2
    inv_freq = ROPE_THETA ** (-jnp.arange(half, dtype=F32) * 2.0 / rot_dim)
    lane = jnp.arange(LANES)
    d = lane % period - lane0
    is_rope = (d >= 0) & (d < rot_dim)
    f_idx = jnp.clip(d, 0, rot_dim - 1) % half
    rows = [
        jnp.where(is_rope, inv_freq[f_idx], 0.0),
        is_rope.astype(F32),
        (is_rope & (d < half)).astype(F32),
        (is_rope & (d >= half)).astype(F32),
    ]
    rows += [jnp.zeros((LANES,), F32)] * 4
    return jnp.stack(rows).astype(F32)


def _rope_tables(pos2d, consts):
    t = pos2d.shape[0]
    tm = 1024
    n = consts.shape[0]
    return pl.pallas_call(
        _rope_tables_kernel,
        grid=(n, t // tm),
        in_specs=[
            pl.BlockSpec((tm, 1), lambda j, i: (i, 0)),
            pl.BlockSpec((None, 8, LANES), lambda j, i: (j, 0, 0)),
        ],
        out_specs=pl.BlockSpec((None, 3, tm, LANES), lambda j, i: (j, 0, i, 0)),
        out_shape=jax.ShapeDtypeStruct((n, 3, t, LANES), F32),
        compiler_params=_cparams("arbitrary", "arbitrary"),
        name="rope_tables",
    )(pos2d, consts)


def _rope_apply(h, t_ref, half):
    cos, sa, sb = t_ref[0], t_ref[1], t_ref[2]
    outs = []
    for j in range(h.shape[1] // LANES):
        xj = h[:, LANES * j:LANES * (j + 1)]
        fwd = pltpu.roll(xj, LANES - half, 1)
        bwd = pltpu.roll(xj, half, 1)
        outs.append(xj * cos + fwd * sa + bwd * sb)
    return outs[0] if len(outs) == 1 else jnp.concatenate(outs, axis=1)


PROJ_COLS = 6 * GROUP_WIDTH + GROUP_WIDTH + 512


def _proj_kernel(x_ref, w_ref, ta_ref, tb_ref,
                 qa_ref, ka_ref, va_ref, qb_ref, kb_ref, vb_ref, hc_ref, hd_ref):
    x = x_ref[...].astype(BF16)

    def seg(j0, n):
        return jnp.dot(x, w_ref[:, j0:j0 + n], preferred_element_type=F32)

    g = GROUP_WIDTH
    qa = _rope_apply(seg(0, g), ta_ref, 4) * (LOG2E * DIFF_QK_DIM ** -0.5)
    qa_ref[...] = qa.T.astype(BF16)
    ka_ref[...] = _rope_apply(seg(g, g), ta_ref, 4).astype(BF16)
    _store_values_t(seg(2 * g, g), va_ref)
    qb = _rope_apply(seg(3 * g, g), tb_ref, 8) * (LOG2E * HEAD_DIM ** -0.5)
    qb_ref[...] = qb.T.astype(BF16)
    kb_ref[...] = _rope_apply(seg(4 * g, g), tb_ref, 8).astype(BF16)
    _store_values_t(seg(5 * g, g), vb_ref)
    hc_ref[...] = seg(6 * g, g)
    hd_ref[...] = seg(7 * g, 512)


def _store_values_t(v, vt_ref):
    tk = vt_ref.shape[3]
    vt = v.astype(BF16).T
    ones = jnp.ones((SUM_ROWS, tk), BF16)
    for h in range(4):
        for c in range(vt_ref.shape[1]):
            vt_ref[h, c, 0:HEAD_DIM, :] = vt[HEAD_DIM * h:HEAD_DIM * (h + 1), tk * c:tk * (c + 1)]
            vt_ref[h, c, HEAD_DIM:VT_ROWS, :] = ones


def _token_spec(tm, n):
    return pl.BlockSpec((None, tm, n), lambda bi, i: (bi, i, 0))


def _feature_spec(tm, n):
    return pl.BlockSpec((None, n, tm), lambda bi, i: (bi, 0, i))


def _values_spec(tm, tk):
    return pl.BlockSpec((None, 4, tm // tk, VT_ROWS, tk), lambda bi, i: (bi, 0, i, 0, 0))


def _values_shape(b, s, tk):
    return jax.ShapeDtypeStruct((b, 4, s // tk, VT_ROWS, tk), BF16)


def _table_spec(tm, s):
    return pl.BlockSpec((3, tm, LANES), lambda bi, i: (0, bi * (s // tm) + i, 0))


def _proj(x3, w, ta, tb):
    b, s, d = x3.shape
    tm = PROJ_TM
    g = GROUP_WIDTH
    tok = lambda n, dt: jax.ShapeDtypeStruct((b, s, n), dt)
    feat = jax.ShapeDtypeStruct((b, g, s), BF16)
    return pl.pallas_call(
        _proj_kernel,
        grid=(b, s // tm),
        in_specs=[_token_spec(tm, d), pl.BlockSpec((d, PROJ_COLS), lambda bi, i: (0, 0)),
                  _table_spec(tm, s), _table_spec(tm, s)],
        out_specs=[_feature_spec(tm, g), _token_spec(tm, g), _values_spec(tm, ATT_TK),
                   _feature_spec(tm, g), _token_spec(tm, g), _values_spec(tm, DIL_TK),
                   _token_spec(tm, g), _token_spec(tm, 512)],
        out_shape=[feat, tok(g, BF16), _values_shape(b, s, ATT_TK),
                   feat, tok(g, BF16), _values_shape(b, s, DIL_TK),
                   tok(g, F32), tok(512, F32)],
        compiler_params=_cparams("arbitrary", "arbitrary"),
        name="proj_rope",
    )(x3, w, ta, tb)


def _rms(x, g):
    return x * lax.rsqrt(jnp.mean(x * x, axis=-1, keepdims=True) + RMS_EPS) * g


def _mla_prep_kernel(hd_ref, td_ref, gq_ref, gkv_ref, wq_ref, wk_ref, wv_ref,
                     q_ref, k_ref, v_ref):
    hd = hd_ref[...]
    cq = _rms(hd[:, 0:MLA_Q_RANK], gq_ref[...]).astype(BF16)
    ckv = _rms(hd[:, MLA_Q_RANK:MLA_Q_RANK + MLA_KV_RANK], gkv_ref[...]).astype(BF16)
    q = jnp.dot(cq, wq_ref[...], preferred_element_type=F32)
    scale = LOG2E * (MLA_NOPE_DIM + MLA_ROPE_DIM) ** -0.5
    q_ref[...] = (_rope_apply(q, td_ref, MLA_ROPE_DIM // 2) * scale).T.astype(BF16)
    k_rope = _rope_apply(hd[:, 384:512], td_ref, MLA_ROPE_DIM // 2)
    k = jnp.dot(ckv, wk_ref[...], preferred_element_type=F32)
    k_ref[...] = (k + jnp.concatenate([k_rope] * 4, axis=1)).astype(BF16)
    _store_values_t(jnp.dot(ckv, wv_ref[...], preferred_element_type=F32), v_ref)


def _mla_prep(hd, td, gq, gkv, wq, wk, wv):
    b, s, _ = hd.shape
    tm = PROJ_TM
    full = lambda a: pl.BlockSpec(a.shape, lambda bi, i: (0,) * a.ndim)
    return pl.pallas_call(
        _mla_prep_kernel,
        grid=(b, s // tm),
        in_specs=[_token_spec(tm, 512), _table_spec(tm, s),
                  full(gq), full(gkv), full(wq), full(wk), full(wv)],
        out_specs=[_feature_spec(tm, 512), _token_spec(tm, 512), _values_spec(tm, ATT_TK)],
        out_shape=[jax.ShapeDtypeStruct((b, 512, s), BF16),
                   jax.ShapeDtypeStruct((b, s, 512), BF16), _values_shape(b, s, ATT_TK)],
        compiler_params=_cparams("arbitrary", "arbitrary"),
        name="mla_prep",
    )(hd, td, gq, gkv, wq, wk, wv)


def _flash_sweep(n_chunks, k_chunk, w, v_chunks, widths, s_ref, p_ref, mask_chunk=None):
    n = w.shape[1]
    offs = [sum(widths[:g]) for g in range(len(widths))]
    last = n_chunks - 1

    def scores(c, slot):
        s = jnp.dot(k_chunk(c), w, preferred_element_type=F32)
        if mask_chunk is not None:
            s = jnp.where(mask_chunk(c) > 0.0, s, NEG_INF)
        s_ref[slot] = s
        return jnp.max(s, axis=0, keepdims=True)

    def values(c, slot, accs):
        return tuple(
            acc + jnp.dot(v_chunks[g](c), p_ref[slot, :, offs[g]:offs[g] + widths[g]],
                          preferred_element_type=F32)
            for g, acc in enumerate(accs))

    def step(c, slot, carry):
        m, cmaxes, accs = carry
        m_new = jnp.maximum(m, cmaxes[0])
        alpha = jnp.exp2(m - m_new)
        cmax_new = scores(jnp.minimum(c + SWEEP_AHEAD, last), (slot + SWEEP_AHEAD) % SWEEP_SLOTS)
        accs = values(jnp.maximum(c - 1, 0), (slot + 1) % 2, accs)
        accs = tuple(acc * alpha[:, offs[g]:offs[g] + widths[g]] for g, acc in enumerate(accs))
        p = jnp.exp2(s_ref[slot] - m_new)
        if mask_chunk is not None:
            p = p * mask_chunk(c)
        p_ref[slot % 2] = p.astype(BF16)
        return m_new, cmaxes[1:] + (cmax_new,), accs

    p_ref[1] = jnp.zeros(p_ref.shape[1:], BF16)
    carry = (jnp.full((1, n), NEG_INF, F32),
             tuple(scores(min(i, last), i) for i in range(SWEEP_AHEAD)),
             tuple(jnp.zeros((VT_ROWS, wd), F32) for wd in widths))

    def trip(j, carry):
        for i in range(SWEEP_STEPS):
            carry = step(SWEEP_STEPS * j + i, i % SWEEP_SLOTS, carry)
        return carry

    looped = n_chunks // SWEEP_STEPS * SWEEP_STEPS
    carry = lax.fori_loop(0, n_chunks // SWEEP_STEPS, trip, carry)
    for c in range(looped, n_chunks):
        carry = step(c, c % SWEEP_SLOTS, carry)
    return values(last, last % 2, carry[2])


SWEEP_SLOTS = 4
SWEEP_AHEAD = 2
SWEEP_STEPS = 8


def _sweep_scratch(tk, n):
    return [pltpu.VMEM((SWEEP_SLOTS, tk, n), F32), pltpu.VMEM((2, tk, n), BF16)]


SAFE_LOG2_SPAN = 100.0
KMAX_ROWS = 1024


def _key_absmax(k_ref, kmax_ref, refresh):
    @pl.when(refresh)
    def _():
        def body(i, m):
            blk = k_ref[pl.ds(pl.multiple_of(i * KMAX_ROWS, KMAX_ROWS), KMAX_ROWS), :]
            return jnp.maximum(m, jnp.max(jnp.abs(blk.astype(F32)), axis=0, keepdims=True))
        m = lax.fori_loop(0, k_ref.shape[0] // KMAX_ROWS, body, jnp.zeros((1, LANES), F32))
        kmax_ref[...] = jnp.broadcast_to(m, kmax_ref.shape)


def _fixed_sweep(n_chunks, k_chunk, w, v_chunks, widths, s_ref, p_ref, m_ref, mask_chunk, first):
    offs = [sum(widths[:g]) for g in range(len(widths))]

    def chunk_of(t):
        return first if t == 0 else (t if t > first else t - 1)

    def probs(s, t, slot):
        p = jnp.exp2(s - m_ref)
        if mask_chunk is not None:
            p = p * mask_chunk(chunk_of(t))
        p_ref[slot] = p.astype(BF16)

    def values(t, slot, accs):
        c = chunk_of(t)
        return tuple(
            acc + jnp.dot(v_chunks[g](c), p_ref[slot, :, offs[g]:offs[g] + widths[g]],
                          preferred_element_type=F32)
            for g, acc in enumerate(accs))

    def step(t, slot, accs):
        s = jnp.dot(k_chunk(chunk_of(t)), w, preferred_element_type=F32)
        accs = values(t - 1, 1 - slot, accs)
        probs(s, t, slot)
        return accs

    probs(s_ref[0], 0, 0)
    accs = tuple(jnp.zeros((VT_ROWS, wd), F32) for wd in widths)

    for t in range(1, n_chunks):
        accs = step(t, t % 2, accs)
    return values(n_chunks - 1, (n_chunks - 1) % 2, accs)


def _softmax_sweep(n_chunks, k_chunk, w, v_chunks, widths, s_ref, p_ref, kmax_ref,
                   mask_chunk=None, first=0):
    s0 = jnp.dot(k_chunk(first), w, preferred_element_type=F32)
    s_ref[0] = s0
    if mask_chunk is not None:
        s0 = jnp.where(mask_chunk(first) > 0.0, s0, NEG_INF)
    m0 = jnp.max(s0, axis=0, keepdims=True)
    kmax = (kmax_ref[...] * 1.01).astype(BF16)
    bound = jnp.dot(kmax, jnp.abs(w), preferred_element_type=F32)[0:1]
    safe = jnp.max(bound - m0) <= SAFE_LOG2_SPAN
    return lax.cond(
        safe,
        lambda: _fixed_sweep(n_chunks, k_chunk, w, v_chunks, widths, s_ref, p_ref, m0,
                             mask_chunk, first),
        lambda: _flash_sweep(n_chunks, k_chunk, w, v_chunks, widths, s_ref, p_ref, mask_chunk))


def _diff_kernel(lam_ref, g_ref, qt_ref, k_ref, vt_ref, o_ref, s_ref, p_ref, kmax_ref, *,
                 lam_init, n_chunks):
    tq = qt_ref.shape[1]
    tk = vt_ref.shape[3]
    lp = lam_ref[...]
    lam = (jnp.exp(jnp.sum(lp[0:1] * lp[1:2], axis=1, keepdims=True))
           - jnp.exp(jnp.sum(lp[2:3] * lp[3:4], axis=1, keepdims=True)) + lam_init)

    qt = qt_ref[...].astype(F32)
    row = lax.broadcasted_iota(jnp.int32, (LANES, tq), 0)
    cols = [jnp.where((row >= DIFF_QK_DIM * c) & (row < DIFF_QK_DIM * (c + 1)), qt, 0.0)
            for c in range(4)]
    w = jnp.concatenate(cols, axis=1).astype(BF16)

    _key_absmax(k_ref, kmax_ref, pl.program_id(2) == 0)
    accs = _softmax_sweep(
        n_chunks,
        lambda c: k_ref[pl.ds(pl.multiple_of(c * tk, tk), tk), :],
        w,
        [lambda c: vt_ref[0, c], lambda c: vt_ref[1, c]],
        [2 * tq, 2 * tq], s_ref, p_ref, kmax_ref)

    gain = g_ref[...] * (1.0 - lam_init)
    for hl, acc in enumerate(accs):
        o1 = acc[0:HEAD_DIM, 0:tq] / acc[HEAD_DIM:HEAD_DIM + 1, 0:tq]
        o2 = acc[0:HEAD_DIM, tq:2 * tq] / acc[HEAD_DIM:HEAD_DIM + 1, tq:2 * tq]
        o = o1 - lam * o2
        ms = jnp.mean(o * o, axis=0, keepdims=True)
        y = o * lax.rsqrt(ms + RMS_EPS) * gain
        o_ref[HEAD_DIM * hl:HEAD_DIM * (hl + 1), :] = y.astype(BF16)


def _diff_attention(lam_params, subln_g, qt, k, vt, layer_idx):
    b, _, s = qt.shape
    tq, tk = DIFF_TQ, ATT_TK
    nck = s // tk
    lam_init = 0.8 - 0.6 * math.exp(-0.3 * layer_idx)
    kern = functools.partial(_diff_kernel, lam_init=lam_init, n_chunks=nck)
    return pl.pallas_call(
        kern,
        grid=(b, 2, s // tq),
        in_specs=[
            pl.BlockSpec((4, DIFF_QK_DIM), lambda bi, j, qi: (0, 0)),
            pl.BlockSpec((HEAD_DIM, 1), lambda bi, j, qi: (0, 0)),
            pl.BlockSpec((None, LANES, tq), lambda bi, j, qi: (bi, j, qi)),
            pl.BlockSpec((None, s, LANES), lambda bi, j, qi: (bi, 0, j)),
            pl.BlockSpec((None, 2, nck, VT_ROWS, tk), lambda bi, j, qi: (bi, j, 0, 0, 0)),
        ],
        out_specs=pl.BlockSpec((None, LANES, tq), lambda bi, j, qi: (bi, j, qi)),
        out_shape=jax.ShapeDtypeStruct((b, GROUP_WIDTH, s), BF16),
        scratch_shapes=_sweep_scratch(tk, 4 * tq) + [pltpu.VMEM((8, LANES), F32)],
        compiler_params=_cparams("arbitrary", "arbitrary", "arbitrary"),
        name="diff_attention",
    )(lam_params, subln_g, qt, k, vt)


def _mla_kernel(qt_ref, k_ref, vt_ref, o_ref, s_ref, p_ref, kmax_ref, *, n_chunks):
    tq = qt_ref.shape[1]
    tk = vt_ref.shape[2]
    _key_absmax(k_ref, kmax_ref, pl.program_id(2) == 0)
    (acc,) = _softmax_sweep(
        n_chunks,
        lambda c: k_ref[pl.ds(pl.multiple_of(c * tk, tk), tk), :],
        qt_ref[...],
        [lambda c: vt_ref[c]],
        [tq], s_ref, p_ref, kmax_ref)
    o_ref[...] = (acc[0:HEAD_DIM] / acc[HEAD_DIM:HEAD_DIM + 1]).astype(BF16)


def _mla_attention(qt, k, vt):
    b, _, s = qt.shape
    tq, tk = MLA_TQ, ATT_TK
    nck = s // tk
    return pl.pallas_call(
        functools.partial(_mla_kernel, n_chunks=nck),
        grid=(b, 4, s // tq),
        in_specs=[
            pl.BlockSpec((None, LANES, tq), lambda bi, h, qi: (bi, h, qi)),
            pl.BlockSpec((None, s, LANES), lambda bi, h, qi: (bi, 0, h)),
            pl.BlockSpec((None, None, nck, VT_ROWS, tk), lambda bi, h, qi: (bi, h, 0, 0, 0)),
        ],
        out_specs=pl.BlockSpec((None, HEAD_DIM, tq), lambda bi, h, qi: (bi, h, qi)),
        out_shape=jax.ShapeDtypeStruct((b, GROUP_WIDTH, s), BF16),
        scratch_shapes=_sweep_scratch(tk, tq) + [pltpu.VMEM((8, LANES), F32)],
        compiler_params=_cparams("arbitrary", "arbitrary", "arbitrary"),
        name="mla_attention",
    )(qt, k, vt)


def _dil_multiplicity(tq, tk):
    n_win = (tq + 2 * DIL_HALO) // tk
    key = jnp.arange(n_win * tk)[:, None] - DIL_HALO
    delta = key - jnp.arange(tq)[None, :]
    mult = jnp.zeros(delta.shape, F32)
    for window, dil in DIL_PATTERNS:
        reach = (window // 2 // dil) * dil
        mult = mult + ((delta % dil == 0) & (jnp.abs(delta) <= reach)).astype(F32)
    return mult.reshape(n_win, tk, tq)


def _dil_kernel(c_ref, qt_ref, k_ref, vt_ref, o_ref, s_ref, p_ref, kmax_ref, *, n_seq_chunks):
    tq = qt_ref.shape[1]
    n_win, tk = c_ref.shape[0], c_ref.shape[1]
    halo_chunks = DIL_HALO // tk
    qi = pl.program_id(2)

    qt = qt_ref[...].astype(F32)
    row = lax.broadcasted_iota(jnp.int32, (LANES, tq), 0)
    w = jnp.concatenate([jnp.where(row < HEAD_DIM, qt, 0.0),
                         jnp.where(row >= HEAD_DIM, qt, 0.0)], axis=1).astype(BF16)

    def seq_chunk(c):
        return qi * (tq // tk) + c - halo_chunks

    def clamped(c):
        return jnp.clip(seq_chunk(c), 0, n_seq_chunks - 1)

    def mask_chunk(c):
        ck = seq_chunk(c)
        inside = jnp.where((ck >= 0) & (ck < n_seq_chunks), 1.0, 0.0)
        mult = c_ref[c] * inside
        return jnp.concatenate([mult, mult], axis=1)

    def k_chunk(c):
        return k_ref[pl.ds(pl.multiple_of(clamped(c) * tk, tk), tk), :]

    _key_absmax(k_ref, kmax_ref, qi == 0)
    accs = _softmax_sweep(
        n_win, k_chunk, w,
        [lambda c: vt_ref[0, clamped(c)], lambda c: vt_ref[1, clamped(c)]],
        [tq, tq], s_ref, p_ref, kmax_ref, mask_chunk=mask_chunk, first=n_win // 2)
    for hl, acc in enumerate(accs):
        o_ref[HEAD_DIM * hl:HEAD_DIM * (hl + 1), :] = (
            acc[0:HEAD_DIM] / acc[HEAD_DIM:HEAD_DIM + 1]).astype(BF16)


def _dil_attention(mult, qt, k, vt):
    b, _, s = qt.shape
    tq, tk = DIL_TQ, DIL_TK
    n_win = mult.shape[0]
    return pl.pallas_call(
        functools.partial(_dil_kernel, n_seq_chunks=s // tk),
        grid=(b, 2, s // tq),
        in_specs=[
            pl.BlockSpec((n_win, tk, tq), lambda bi, j, qi: (0, 0, 0)),
            pl.BlockSpec((None, LANES, tq), lambda bi, j, qi: (bi, j, qi)),
            pl.BlockSpec((None, s, LANES), lambda bi, j, qi: (bi, 0, j)),
            pl.BlockSpec((None, 2, s // tk, VT_ROWS, tk), lambda bi, j, qi: (bi, j, 0, 0, 0)),
        ],
        out_specs=pl.BlockSpec((None, LANES, tq), lambda bi, j, qi: (bi, j, qi)),
        out_shape=jax.ShapeDtypeStruct((b, GROUP_WIDTH, s), BF16),
        scratch_shapes=_sweep_scratch(tk, 2 * tq) + [pltpu.VMEM((8, LANES), F32)],
        compiler_params=_cparams("arbitrary", "arbitrary", "arbitrary"),
        name="dilated_attention",
    )(mult, qt, k, vt)


def _with_halo(prev_ref, cur_ref, next_ref, i, n_tiles):
    prev = jnp.where(i > 0, prev_ref[...], 0.0)
    nxt = jnp.where(i < n_tiles - 1, next_ref[...], 0.0)
    return jnp.concatenate([prev, cur_ref[...], nxt], axis=0)


def _shift_rows(x, k):
    n = x.shape[0]
    return pltpu.roll(x, (n - k) % n, 0)


def _pool_kernel(prev_ref, cur_ref, next_ref, w_ref, scale_ref, o_ref, *, seq_len):
    tm = cur_ref.shape[0]
    i = pl.program_id(1)
    x = _with_halo(prev_ref, cur_ref, next_ref, i, seq_len // tm)
    t = i * tm + lax.broadcasted_iota(jnp.int32, (tm, 1), 0)
    lane_group = lax.broadcasted_iota(jnp.int32, (tm, GROUP_WIDTH), 1) // POOL_GROUP

    run = x
    mean = None
    for g, wnd in enumerate(POOL_WINDOWS):
        run = run + _shift_rows(run, wnd // 2)
        total = _shift_rows(run, -(wnd // 2))[HALO:HALO + tm]
        lo = jnp.clip(t - wnd // 2, 0, seq_len - 1)
        hi = jnp.clip(t + wnd - wnd // 2 - 1, 0, seq_len - 1)
        cand = total / (hi - lo + 1).astype(F32)
        mean = cand if mean is None else jnp.where(lane_group == g, cand, mean)
    d = (mean - cur_ref[...]).astype(BF16)
    y = jnp.dot(d, w_ref[...], preferred_element_type=F32) * scale_ref[...]
    o_ref[...] = y.astype(BF16)


def _halo_specs(tm, cols, seq_len):
    nb = tm // HALO
    last = seq_len // HALO - 1
    return [
        pl.BlockSpec((None, HALO, cols), lambda bi, i: (bi, jnp.maximum(i * nb - 1, 0), 0)),
        pl.BlockSpec((None, tm, cols), lambda bi, i: (bi, i, 0)),
        pl.BlockSpec((None, HALO, cols), lambda bi, i: (bi, jnp.minimum((i + 1) * nb, last), 0)),
    ]


def _pool_mixer(hc3, w_bd, scale):
    b, s, c = hc3.shape
    tm = POOL_TM
    prev_s, cur_s, next_s = _halo_specs(tm, c, s)
    return pl.pallas_call(
        functools.partial(_pool_kernel, seq_len=s),
        grid=(b, s // tm),
        in_specs=[prev_s, cur_s, next_s,
                  pl.BlockSpec(w_bd.shape, lambda bi, i: (0, 0)),
                  pl.BlockSpec(scale.shape, lambda bi, i: (0, 0))],
        out_specs=pl.BlockSpec((None, tm, c), lambda bi, i: (bi, i, 0)),
        out_shape=jax.ShapeDtypeStruct((b, s, c), BF16),
        compiler_params=_cparams("arbitrary", "arbitrary"),
        name="pool_mixer",
    )(hc3, hc3, hc3, w_bd, scale)


def _layer_norm(z, g, b):
    mu = jnp.mean(z, axis=-1, keepdims=True)
    zc = z - mu
    var = jnp.mean(zc * zc, axis=-1, keepdims=True)
    return zc * lax.rsqrt(var + LN_EPS) * g + b


def _tdot(at, w):
    return lax.dot_general(at, w, (((0,), (0,)), ((), ())), preferred_element_type=F32)


def _out_kernel(x_ref, ya_ref, yb_ref, yc_ref, yd_ref, w_ref, g_ref, b_ref, o_ref):
    gw = GROUP_WIDTH
    z = _tdot(ya_ref[...], w_ref[0:gw, :])
    z = z + _tdot(yb_ref[...], w_ref[gw:2 * gw, :])
    z = z + jnp.dot(yc_ref[...], w_ref[2 * gw:3 * gw, :], preferred_element_type=F32)
    z = z + _tdot(yd_ref[...], w_ref[3 * gw:4 * gw, :])
    o_ref[...] = _layer_norm(DN_ALPHA * x_ref[...] + z, g_ref[...], b_ref[...])


def _out_proj(x3, ya_t, yb_t, yc, yd_t, w, g, bias):
    b, s, d = x3.shape
    tm = OUT_TM
    tok = lambda n: pl.BlockSpec((None, tm, n), lambda bi, i: (bi, i, 0))
    feat = pl.BlockSpec((None, GROUP_WIDTH, tm), lambda bi, i: (bi, 0, i))
    full = lambda a: pl.BlockSpec(a.shape, lambda bi, i: (0,) * a.ndim)
    return pl.pallas_call(
        _out_kernel,
        grid=(b, s // tm),
        in_specs=[tok(d), feat, feat, tok(GROUP_WIDTH), feat, full(w), full(g), full(bias)],
        out_specs=tok(d),
        out_shape=jax.ShapeDtypeStruct((b, s, d), F32),
        compiler_params=_cparams("arbitrary", "arbitrary"),
        name="out_proj_ln",
    )(x3, ya_t, yb_t, yc, yd_t, w, g, bias)


def _ffn_kernel(prev_ref, cur_ref, next_ref, wup_ref, wd_ref, cw_ref, g_ref, b_ref, o_ref,
                act_ref, *, seq_len):
    tm = cur_ref.shape[0]
    i = pl.program_id(1)
    xe = _with_halo(prev_ref, cur_ref, next_ref, i, seq_len // tm).astype(BF16)

    def conv_proj(j0):
        u = jnp.dot(xe, wup_ref[:, j0:j0 + FFN_CHUNK], preferred_element_type=F32)
        c = cw_ref[:, j0:j0 + FFN_CHUNK]
        y = _shift_rows(u, -1) * c[0:1] + u * c[1:2] + _shift_rows(u, 1) * c[2:3] + c[3:4]
        return y[HALO:HALO + tm]

    for j0 in range(0, D_FF, FFN_CHUNK):
        gate = conv_proj(j0)
        up = conv_proj(D_FF + j0)
        act_ref[:, j0:j0 + FFN_CHUNK] = (gate * jax.nn.sigmoid(gate) * up).astype(BF16)
    f = jnp.dot(act_ref[...], wd_ref[...], preferred_element_type=F32)
    o_ref[...] = _layer_norm(DN_ALPHA * cur_ref[...] + f, g_ref[...], b_ref[...])


def _ffn(x3, wup, wd, cw, g, bias):
    b, s, d = x3.shape
    tm = FFN_TM
    full = lambda a: pl.BlockSpec(a.shape, lambda bi, i: (0,) * a.ndim)
    return pl.pallas_call(
        functools.partial(_ffn_kernel, seq_len=s),
        grid=(b, s // tm),
        in_specs=_halo_specs(tm, d, s) + [full(wup), full(wd), full(cw), full(g), full(bias)],
        out_specs=pl.BlockSpec((None, tm, d), lambda bi, i: (bi, i, 0)),
        out_shape=jax.ShapeDtypeStruct((b, s, d), F32),
        scratch_shapes=[pltpu.VMEM((tm, D_FF), BF16)],
        compiler_params=_cparams("arbitrary", "arbitrary"),
        name="conv_ffn_ln",
    )(x3, x3, x3, wup, wd, cw, g, bias)


def _pad_cols(w, groups, width, padded):
    k = w.shape[0]
    w = w.reshape(k, groups, width)
    return jnp.pad(w, ((0, 0), (0, 0), (0, padded - width))).reshape(k, groups * padded)


def kernel(x, positions, w_in, diff_lambda, diff_subln, pool_w, pool_scale, mla_q_norm,
           mla_kv_norm, mla_w_uq, mla_w_ukv, w_out, ln1_g, ln1_b, ffn_w_up, ffn_conv_w,
           ffn_conv_b, ffn_w_down, ln2_g, ln2_b):
    b, s, d = x.shape
    t = b * s
    assert d == D_MODEL and s % 1024 == 0

    consts = jnp.stack([
        _rope_consts(DIFF_QK_DIM, DIFF_QK_DIM // 4, 0),
        _rope_consts(HEAD_DIM, HEAD_DIM // 4, 0),
        _rope_consts(LANES, MLA_ROPE_DIM, MLA_NOPE_DIM),
    ])
    tables = _rope_tables(positions.reshape(t, 1), consts)
    ta, tb, td = tables[0], tables[1], tables[2]
    mult = _dil_multiplicity(DIL_TQ, DIL_TK)

    for l in range(DEPTH):
        wi = w_in[l]
        w_proj = jnp.concatenate([
            wi[:, :7 * GROUP_WIDTH + MLA_Q_RANK + MLA_KV_RANK],
            jnp.zeros((d, MLA_NOPE_DIM), wi.dtype),
            wi[:, 7 * GROUP_WIDTH + MLA_Q_RANK + MLA_KV_RANK:],
            jnp.zeros((d, LANES - MLA_NOPE_DIM - MLA_ROPE_DIM), wi.dtype)], axis=1).astype(BF16)
        wq = _pad_cols(mla_w_uq[l], 4, MLA_NOPE_DIM + MLA_ROPE_DIM, LANES).astype(BF16)
        wkv = mla_w_ukv[l].reshape(MLA_KV_RANK, 4, 2, HEAD_DIM)
        wk = _pad_cols(wkv[:, :, 0].reshape(MLA_KV_RANK, 4 * HEAD_DIM), 4, HEAD_DIM, LANES).astype(BF16)
        wv = wkv[:, :, 1].reshape(MLA_KV_RANK, 4 * HEAD_DIM).astype(BF16)
        pw = pool_w[l]
        w_pool = jnp.zeros((GROUP_WIDTH, GROUP_WIDTH), pw.dtype)
        for g in range(4):
            w_pool = w_pool.at[g * POOL_GROUP:(g + 1) * POOL_GROUP,
                               g * POOL_GROUP:(g + 1) * POOL_GROUP].set(pw[g])
        w_pool = w_pool.astype(BF16)
        conv = jnp.concatenate([ffn_conv_w[l], ffn_conv_b[l][None, :],
                                jnp.zeros((4, 2 * D_FF), F32)], axis=0)

        qa_t, ka, va_t, qb_t, kb, vb_t, hc, hd = _proj(x, w_proj, ta, tb)
        qd_t, kd, vd_t = _mla_prep(hd, td, mla_q_norm[l][None, :], mla_kv_norm[l][None, :],
                                   wq, wk, wv)
        ya_t = _diff_attention(diff_lambda[l], diff_subln[l][:, None], qa_t, ka, va_t, l)
        yb_t = _dil_attention(mult, qb_t, kb, vb_t)
        yc = _pool_mixer(hc, w_pool, pool_scale[l][None, :])
        yd_t = _mla_attention(qd_t, kd, vd_t)

        x = _out_proj(x, ya_t, yb_t, yc, yd_t, w_out[l].astype(BF16),
                      ln1_g[l][None, :], ln1_b[l][None, :])
        x = _ffn(x, ffn_w_up[l].astype(BF16), ffn_w_down[l].astype(BF16), conv,
                 ln2_g[l][None, :], ln2_b[l][None, :])
    return x
```

```python
import functools
import math

import jax
import jax.numpy as jnp
from jax import lax
from jax.experimental import pallas as pl
from jax.experimental.pallas import tpu as pltpu

F32 = jnp.float32
BF16 = jnp.bfloat16

D_MODEL = 1024
DEPTH = 4
HEAD_DIM = 64
GROUP_WIDTH = 256
ROPE_THETA = 500000.0
NEG_INF = -1e30
LOG2E = math.log2(math.e)

DIFF_QK_DIM = 32
DIL_PATTERNS = ((128, 1), (512, 4), (2048, 16))
POOL_WINDOWS = (2, 4, 8, 16)
POOL_GROUP = 64
MLA_Q_RANK = 256
MLA_KV_RANK = 128
MLA_NOPE_DIM = 64
MLA_ROPE_DIM = 32
D_FF = 2816
DN_ALPHA = (2 * DEPTH) ** 0.25
LN_EPS = 1e-5
RMS_EPS = 1e-6

LANES = 128
SUM_ROWS = 16
VT_ROWS = HEAD_DIM + SUM_ROWS
DIL_HALO = 1024
VMEM_LIMIT = 56 * 1024 * 1024

PROJ_TM = 512
ATT_TK = 512
DIFF_TQ = 256
MLA_TQ = 512
DIL_TQ = 256
DIL_TK = 256
POOL_TM = 512
OUT_TM = 512
FFN_TM = 512
FFN_CHUNK = 256
HALO = 8


def _cparams(*sem):
    return pltpu.CompilerParams(dimension_semantics=sem, vmem_limit_bytes=VMEM_LIMIT)


def _rope_tables_kernel(pos_ref, c_ref, o_ref):
    pos = pos_ref[...].astype(F32)
    ang = pos * c_ref[0:1, :]
    cs = jnp.cos(ang)
    sn = jnp.sin(ang)
    is_rope = c_ref[1:2, :]
    o_ref[0] = cs * is_rope + (1.0 - is_rope)
    o_ref[1] = -sn * c_ref[2:3, :]
    o_ref[2] = sn * c_ref[3:4, :]


def _rope_consts(period, rot_dim, lane0):
    half = rot_dim // 2
    inv_freq = ROPE_THETA ** (-jnp.arange(half, dtype=F32) * 2.0 / rot_dim)
    lane = jnp.arange(LANES)
    d = lane % period - lane0
    is_rope = (d >= 0) & (d < rot_dim)
    f_idx = jnp.clip(d, 0, rot_dim - 1) % half
    rows = [
        jnp.where(is_rope, inv_freq[f_idx], 0.0),
        is_rope.astype(F32),
        (is_rope & (d < half)).astype(F32),
        (is_rope & (d >= half)).astype(F32),
    ]
    rows += [jnp.zeros((LANES,), F32)] * 4
    return jnp.stack(rows).astype(F32)


def _rope_tables(pos2d, consts):
    t = pos2d.shape[0]
    tm = 1024
    n = consts.shape[0]
    return pl.pallas_call(
        _rope_tables_kernel,
        grid=(n, t // tm),
        in_specs=[
            pl.BlockSpec((tm, 1), lambda j, i: (i, 0)),
            pl.BlockSpec((None, 8, LANES), lambda j, i: (j, 0, 0)),
        ],
        out_specs=pl.BlockSpec((None, 3, tm, LANES), lambda j, i: (j, 0, i, 0)),
        out_shape=jax.ShapeDtypeStruct((n, 3, t, LANES), F32),
        compiler_params=_cparams("arbitrary", "arbitrary"),
        name="rope_tables",
    )(pos2d, consts)


def _rope_apply(h, t_ref, half):
    cos, sa, sb = t_ref[0], t_ref[1], t_ref[2]
    outs = []
    for j in range(h.shape[1] // LANES):
        xj = h[:, LANES * j:LANES * (j + 1)]
        fwd = pltpu.roll(xj, LANES - half, 1)
        bwd = pltpu.roll(xj, half, 1)
        outs.append(xj * cos + fwd * sa + bwd * sb)
    return outs[0] if len(outs) == 1 else jnp.concatenate(outs, axis=1)


PROJ_COLS = 6 * GROUP_WIDTH + GROUP_WIDTH + 512


def _proj_kernel(x_ref, w_ref, ta_ref, tb_ref,
                 qa_ref, ka_ref, va_ref, qb_ref, kb_ref, vb_ref, hc_ref, hd_ref):
    x = x_ref[...].astype(BF16)

    def seg(j0, n):
        return jnp.dot(x, w_ref[:, j0:j0 + n], preferred_element_type=F32)

    g = GROUP_WIDTH
    qa = _rope_apply(seg(0, g), ta_ref, 4) * (LOG2E * DIFF_QK_DIM ** -0.5)
    qa_ref[...] = qa.T.astype(BF16)
    ka_ref[...] = _rope_apply(seg(g, g), ta_ref, 4).astype(BF16)
    _store_values_t(seg(2 * g, g), va_ref)
    qb = _rope_apply(seg(3 * g, g), tb_ref, 8) * (LOG2E * HEAD_DIM ** -0.5)
    qb_ref[...] = qb.T.astype(BF16)
    kb_ref[...] = _rope_apply(seg(4 * g, g), tb_ref, 8).astype(BF16)
    _store_values_t(seg(5 * g, g), vb_ref)
    hc_ref[...] = seg(6 * g, g)
    hd_ref[...] = seg(7 * g, 512)


def _store_values_t(v, vt_ref):
    tk = vt_ref.shape[3]
    vt = v.astype(BF16).T
    ones = jnp.ones((SUM_ROWS, tk), BF16)
    for h in range(4):
        for c in range(vt_ref.shape[1]):
            vt_ref[h, c, 0:HEAD_DIM, :] = vt[HEAD_DIM * h:HEAD_DIM * (h + 1), tk * c:tk * (c + 1)]
            vt_ref[h, c, HEAD_DIM:VT_ROWS, :] = ones


def _token_spec(tm, n):
    return pl.BlockSpec((None, tm, n), lambda bi, i: (bi, i, 0))


def _feature_spec(tm, n):
    return pl.BlockSpec((None, n, tm), lambda bi, i: (bi, 0, i))


def _values_spec(tm, tk):
    return pl.BlockSpec((None, 4, tm // tk, VT_ROWS, tk), lambda bi, i: (bi, 0, i, 0, 0))


def _values_shape(b, s, tk):
    return jax.ShapeDtypeStruct((b, 4, s // tk, VT_ROWS, tk), BF16)


def _table_spec(tm, s):
    return pl.BlockSpec((3, tm, LANES), lambda bi, i: (0, bi * (s // tm) + i, 0))


def _proj(x3, w, ta, tb):
    b, s, d = x3.shape
    tm = PROJ_TM
    g = GROUP_WIDTH
    tok = lambda n, dt: jax.ShapeDtypeStruct((b, s, n), dt)
    feat = jax.ShapeDtypeStruct((b, g, s), BF16)
    return pl.pallas_call(
        _proj_kernel,
        grid=(b, s // tm),
        in_specs=[_token_spec(tm, d), pl.BlockSpec((d, PROJ_COLS), lambda bi, i: (0, 0)),
                  _table_spec(tm, s), _table_spec(tm, s)],
        out_specs=[_feature_spec(tm, g), _token_spec(tm, g), _values_spec(tm, ATT_TK),
                   _feature_spec(tm, g), _token_spec(tm, g), _values_spec(tm, DIL_TK),
                   _token_spec(tm, g), _token_spec(tm, 512)],
        out_shape=[feat, tok(g, BF16), _values_shape(b, s, ATT_TK),
                   feat, tok(g, BF16), _values_shape(b, s, DIL_TK),
                   tok(g, F32), tok(512, F32)],
        compiler_params=_cparams("arbitrary", "arbitrary"),
        name="proj_rope",
    )(x3, w, ta, tb)


def _rms(x, g):
    return x * lax.rsqrt(jnp.mean(x * x, axis=-1, keepdims=True) + RMS_EPS) * g


def _mla_prep_kernel(hd_ref, td_ref, gq_ref, gkv_ref, wq_ref, wk_ref, wv_ref,
                     q_ref, k_ref, v_ref):
    hd = hd_ref[...]
    cq = _rms(hd[:, 0:MLA_Q_RANK], gq_ref[...]).astype(BF16)
    ckv = _rms(hd[:, MLA_Q_RANK:MLA_Q_RANK + MLA_KV_RANK], gkv_ref[...]).astype(BF16)
    q = jnp.dot(cq, wq_ref[...], preferred_element_type=F32)
    scale = LOG2E * (MLA_NOPE_DIM + MLA_ROPE_DIM) ** -0.5
    q_ref[...] = (_rope_apply(q, td_ref, MLA_ROPE_DIM // 2) * scale).T.astype(BF16)
    k_rope = _rope_apply(hd[:, 384:512], td_ref, MLA_ROPE_DIM // 2)
    k = jnp.dot(ckv, wk_ref[...], preferred_element_type=F32)
    k_ref[...] = (k + jnp.concatenate([k_rope] * 4, axis=1)).astype(BF16)
    _store_values_t(jnp.dot(ckv, wv_ref[...], preferred_element_type=F32), v_ref)


def _mla_prep(hd, td, gq, gkv, wq, wk, wv):
    b, s, _ = hd.shape
    tm = PROJ_TM
    full = lambda a: pl.BlockSpec(a.shape, lambda bi, i: (0,) * a.ndim)
    return pl.pallas_call(
        _mla_prep_kernel,
        grid=(b, s // tm),
        in_specs=[_token_spec(tm, 512), _table_spec(tm, s),
                  full(gq), full(gkv), full(wq), full(wk), full(wv)],
        out_specs=[_feature_spec(tm, 512), _token_spec(tm, 512), _values_spec(tm, ATT_TK)],
        out_shape=[jax.ShapeDtypeStruct((b, 512, s), BF16),
                   jax.ShapeDtypeStruct((b, s, 512), BF16), _values_shape(b, s, ATT_TK)],
        compiler_params=_cparams("arbitrary", "arbitrary"),
        name="mla_prep",
    )(hd, td, gq, gkv, wq, wk, wv)


def _flash_sweep(n_chunks, k_chunk, w, v_chunks, widths, s_ref, p_ref, mask_chunk=None):
    n = w.shape[1]
    offs = [sum(widths[:g]) for g in range(len(widths))]
    last = n_chunks - 1

    def scores(c, slot):
        s = jnp.dot(k_chunk(c), w, preferred_element_type=F32)
        if mask_chunk is not None:
            s = jnp.where(mask_chunk(c) > 0.0, s, NEG_INF)
        s_ref[slot] = s
        return jnp.max(s, axis=0, keepdims=True)

    def values(c, slot, accs):
        return tuple(
            acc + jnp.dot(v_chunks[g](c), p_ref[slot, :, offs[g]:offs[g] + widths[g]],
                          preferred_element_type=F32)
            for g, acc in enumerate(accs))

    def step(c, slot, carry):
        m, cmaxes, accs = carry
        m_new = jnp.maximum(m, cmaxes[0])
        alpha = jnp.exp2(m - m_new)
        cmax_new = scores(jnp.minimum(c + SWEEP_AHEAD, last), (slot + SWEEP_AHEAD) % SWEEP_SLOTS)
        accs = values(jnp.maximum(c - 1, 0), (slot + 1) % 2, accs)
        accs = tuple(acc * alpha[:, offs[g]:offs[g] + widths[g]] for g, acc in enumerate(accs))
        p = jnp.exp2(s_ref[slot] - m_new)
        if mask_chunk is not None:
            p = p * mask_chunk(c)
        p_ref[slot % 2] = p.astype(BF16)
        return m_new, cmaxes[1:] + (cmax_new,), accs

    p_ref[1] = jnp.zeros(p_ref.shape[1:], BF16)
    carry = (jnp.full((1, n), NEG_INF, F32),
             tuple(scores(min(i, last), i) for i in range(SWEEP_AHEAD)),
             tuple(jnp.zeros((VT_ROWS, wd), F32) for wd in widths))

    def trip(j, carry):
        for i in range(SWEEP_STEPS):
            carry = step(SWEEP_STEPS * j + i, i % SWEEP_SLOTS, carry)
        return carry

    looped = n_chunks // SWEEP_STEPS * SWEEP_STEPS
    carry = lax.fori_loop(0, n_chunks // SWEEP_STEPS, trip, carry)
    for c in range(looped, n_chunks):
        carry = step(c, c % SWEEP_SLOTS, carry)
    return values(last, last % 2, carry[2])


SWEEP_SLOTS = 4
SWEEP_AHEAD = 2
SWEEP_STEPS = 8


def _sweep_scratch(tk, n):
    return [pltpu.VMEM((SWEEP_SLOTS, tk, n), F32), pltpu.VMEM((P_SLOTS, tk, n), BF16)]


SAFE_LOG2_SPAN = 100.0
KMAX_ROWS = 1024


def _key_absmax(k_ref, kmax_ref, refresh):
    @pl.when(refresh)
    def _():
        def body(i, m):
            blk = k_ref[pl.ds(pl.multiple_of(i * KMAX_ROWS, KMAX_ROWS), KMAX_ROWS), :]
            return jnp.maximum(m, jnp.max(jnp.abs(blk.astype(F32)), axis=0, keepdims=True))
        m = lax.fori_loop(0, k_ref.shape[0] // KMAX_ROWS, body, jnp.zeros((1, LANES), F32))
        kmax_ref[...] = jnp.broadcast_to(m, kmax_ref.shape)


def _fixed_sweep(n_chunks, k_chunk, w, v_chunks, widths, s_ref, p_ref, m_ref, mask_chunk, first):
    offs = [sum(widths[:g]) for g in range(len(widths))]

    def chunk_of(t):
        return first if t == 0 else (t if t > first else t - 1)

    def probs(s, t):
        p = jnp.exp2(s - m_ref)
        if mask_chunk is not None:
            p = p * mask_chunk(chunk_of(t))
        p_ref[t % P_SLOTS] = p.astype(BF16)

    def values(t, accs):
        c = chunk_of(t)
        return tuple(
            acc + jnp.dot(v_chunks[g](c), p_ref[t % P_SLOTS, :, offs[g]:offs[g] + widths[g]],
                          preferred_element_type=F32)
            for g, acc in enumerate(accs))

    probs(s_ref[0], 0)
    accs = tuple(jnp.zeros((VT_ROWS, wd), F32) for wd in widths)

    for t in range(1, n_chunks):
        s = jnp.dot(k_chunk(chunk_of(t)), w, preferred_element_type=F32)
        if t >= PV_LAG:
            accs = values(t - PV_LAG, accs)
        probs(s, t)
    for t in range(max(n_chunks - PV_LAG, 0), n_chunks):
        accs = values(t, accs)
    return accs


PV_LAG = 2
P_SLOTS = PV_LAG + 1


def _softmax_sweep(n_chunks, k_chunk, w, v_chunks, widths, s_ref, p_ref, kmax_ref,
                   mask_chunk=None, first=0):
    s0 = jnp.dot(k_chunk(first), w, preferred_element_type=F32)
    s_ref[0] = s0
    if mask_chunk is not None:
        s0 = jnp.where(mask_chunk(first) > 0.0, s0, NEG_INF)
    m0 = jnp.max(s0, axis=0, keepdims=True)
    kmax = (kmax_ref[...] * 1.01).astype(BF16)
    bound = jnp.dot(kmax, jnp.abs(w), preferred_element_type=F32)[0:1]
    safe = jnp.max(bound - m0) <= SAFE_LOG2_SPAN
    return lax.cond(
        safe,
        lambda: _fixed_sweep(n_chunks, k_chunk, w, v_chunks, widths, s_ref, p_ref, m0,
                             mask_chunk, first),
        lambda: _flash_sweep(n_chunks, k_chunk, w, v_chunks, widths, s_ref, p_ref, mask_chunk))


def _diff_kernel(lam_ref, g_ref, qt_ref, k_ref, vt_ref, o_ref, s_ref, p_ref, kmax_ref, *,
                 lam_init, n_chunks):
    tq = qt_ref.shape[1]
    tk = vt_ref.shape[3]
    lp = lam_ref[...]
    lam = (jnp.exp(jnp.sum(lp[0:1] * lp[1:2], axis=1, keepdims=True))
           - jnp.exp(jnp.sum(lp[2:3] * lp[3:4], axis=1, keepdims=True)) + lam_init)

    qt = qt_ref[...].astype(F32)
    row = lax.broadcasted_iota(jnp.int32, (LANES, tq), 0)
    cols = [jnp.where((row >= DIFF_QK_DIM * c) & (row < DIFF_QK_DIM * (c + 1)), qt, 0.0)
            for c in range(4)]
    w = jnp.concatenate(cols, axis=1).astype(BF16)

    _key_absmax(k_ref, kmax_ref, pl.program_id(2) == 0)
    accs = _softmax_sweep(
        n_chunks,
        lambda c: k_ref[pl.ds(pl.multiple_of(c * tk, tk), tk), :],
        w,
        [lambda c: vt_ref[0, c], lambda c: vt_ref[1, c]],
        [2 * tq, 2 * tq], s_ref, p_ref, kmax_ref)

    gain = g_ref[...] * (1.0 - lam_init)
    for hl, acc in enumerate(accs):
        o1 = acc[0:HEAD_DIM, 0:tq] / acc[HEAD_DIM:HEAD_DIM + 1, 0:tq]
        o2 = acc[0:HEAD_DIM, tq:2 * tq] / acc[HEAD_DIM:HEAD_DIM + 1, tq:2 * tq]
        o = o1 - lam * o2
        ms = jnp.mean(o * o, axis=0, keepdims=True)
        y = o * lax.rsqrt(ms + RMS_EPS) * gain
        o_ref[HEAD_DIM * hl:HEAD_DIM * (hl + 1), :] = y.astype(BF16)


def _diff_attention(lam_params, subln_g, qt, k, vt, layer_idx):
    b, _, s = qt.shape
    tq, tk = DIFF_TQ, ATT_TK
    nck = s // tk
    lam_init = 0.8 - 0.6 * math.exp(-0.3 * layer_idx)
    kern = functools.partial(_diff_kernel, lam_init=lam_init, n_chunks=nck)
    return pl.pallas_call(
        kern,
        grid=(b, 2, s // tq),
        in_specs=[
            pl.BlockSpec((4, DIFF_QK_DIM), lambda bi, j, qi: (0, 0)),
            pl.BlockSpec((HEAD_DIM, 1), lambda bi, j, qi: (0, 0)),
            pl.BlockSpec((None, LANES, tq), lambda bi, j, qi: (bi, j, qi)),
            pl.BlockSpec((None, s, LANES), lambda bi, j, qi: (bi, 0, j)),
            pl.BlockSpec((None, 2, nck, VT_ROWS, tk), lambda bi, j, qi: (bi, j, 0, 0, 0)),
        ],
        out_specs=pl.BlockSpec((None, LANES, tq), lambda bi, j, qi: (bi, j, qi)),
        out_shape=jax.ShapeDtypeStruct((b, GROUP_WIDTH, s), BF16),
        scratch_shapes=_sweep_scratch(tk, 4 * tq) + [pltpu.VMEM((8, LANES), F32)],
        compiler_params=_cparams("arbitrary", "arbitrary", "arbitrary"),
        name="diff_attention",
    )(lam_params, subln_g, qt, k, vt)


def _mla_kernel(qt_ref, k_ref, vt_ref, o_ref, s_ref, p_ref, kmax_ref, *, n_chunks):
    tq = qt_ref.shape[1]
    tk = vt_ref.shape[2]
    _key_absmax(k_ref, kmax_ref, pl.program_id(2) == 0)
    (acc,) = _softmax_sweep(
        n_chunks,
        lambda c: k_ref[pl.ds(pl.multiple_of(c * tk, tk), tk), :],
        qt_ref[...],
        [lambda c: vt_ref[c]],
        [tq], s_ref, p_ref, kmax_ref)
    o_ref[...] = (acc[0:HEAD_DIM] / acc[HEAD_DIM:HEAD_DIM + 1]).astype(BF16)


def _mla_attention(qt, k, vt):
    b, _, s = qt.shape
    tq, tk = MLA_TQ, ATT_TK
    nck = s // tk
    return pl.pallas_call(
        functools.partial(_mla_kernel, n_chunks=nck),
        grid=(b, 4, s // tq),
        in_specs=[
            pl.BlockSpec((None, LANES, tq), lambda bi, h, qi: (bi, h, qi)),
            pl.BlockSpec((None, s, LANES), lambda bi, h, qi: (bi, 0, h)),
            pl.BlockSpec((None, None, nck, VT_ROWS, tk), lambda bi, h, qi: (bi, h, 0, 0, 0)),
        ],
        out_specs=pl.BlockSpec((None, HEAD_DIM, tq), lambda bi, h, qi: (bi, h, qi)),
        out_shape=jax.ShapeDtypeStruct((b, GROUP_WIDTH, s), BF16),
        scratch_shapes=_sweep_scratch(tk, tq) + [pltpu.VMEM((8, LANES), F32)],
        compiler_params=_cparams("arbitrary", "arbitrary", "arbitrary"),
        name="mla_attention",
    )(qt, k, vt)


def _dil_multiplicity(tq, tk):
    n_win = (tq + 2 * DIL_HALO) // tk
    key = jnp.arange(n_win * tk)[:, None] - DIL_HALO
    delta = key - jnp.arange(tq)[None, :]
    mult = jnp.zeros(delta.shape, F32)
    for window, dil in DIL_PATTERNS:
        reach = (window // 2 // dil) * dil
        mult = mult + ((delta % dil == 0) & (jnp.abs(delta) <= reach)).astype(F32)
    return mult.reshape(n_win, tk, tq)


def _dil_kernel(c_ref, qt_ref, k_ref, vt_ref, o_ref, s_ref, p_ref, kmax_ref, *, n_seq_chunks):
    tq = qt_ref.shape[1]
    n_win, tk = c_ref.shape[0], c_ref.shape[1]
    halo_chunks = DIL_HALO // tk
    qi = pl.program_id(2)

    qt = qt_ref[...].astype(F32)
    row = lax.broadcasted_iota(jnp.int32, (LANES, tq), 0)
    w = jnp.concatenate([jnp.where(row < HEAD_DIM, qt, 0.0),
                         jnp.where(row >= HEAD_DIM, qt, 0.0)], axis=1).astype(BF16)

    def seq_chunk(c):
        return qi * (tq // tk) + c - halo_chunks

    def clamped(c):
        return jnp.clip(seq_chunk(c), 0, n_seq_chunks - 1)

    def mask_chunk(c):
        ck = seq_chunk(c)
        inside = jnp.where((ck >= 0) & (ck < n_seq_chunks), 1.0, 0.0)
        mult = c_ref[c] * inside
        return jnp.concatenate([mult, mult], axis=1)

    def k_chunk(c):
        return k_ref[pl.ds(pl.multiple_of(clamped(c) * tk, tk), tk), :]

    _key_absmax(k_ref, kmax_ref, qi == 0)
    accs = _softmax_sweep(
        n_win, k_chunk, w,
        [lambda c: vt_ref[0, clamped(c)], lambda c: vt_ref[1, clamped(c)]],
        [tq, tq], s_ref, p_ref, kmax_ref, mask_chunk=mask_chunk, first=n_win // 2)
    for hl, acc in enumerate(accs):
        o_ref[HEAD_DIM * hl:HEAD_DIM * (hl + 1), :] = (
            acc[0:HEAD_DIM] / acc[HEAD_DIM:HEAD_DIM + 1]).astype(BF16)


def _dil_attention(mult, qt, k, vt):
    b, _, s = qt.shape
    tq, tk = DIL_TQ, DIL_TK
    n_win = mult.shape[0]
    return pl.pallas_call(
        functools.partial(_dil_kernel, n_seq_chunks=s // tk),
        grid=(b, 2, s // tq),
        in_specs=[
            pl.BlockSpec((n_win, tk, tq), lambda bi, j, qi: (0, 0, 0)),
            pl.BlockSpec((None, LANES, tq), lambda bi, j, qi: (bi, j, qi)),
            pl.BlockSpec((None, s, LANES), lambda bi, j, qi: (bi, 0, j)),
            pl.BlockSpec((None, 2, s // tk, VT_ROWS, tk), lambda bi, j, qi: (bi, j, 0, 0, 0)),
        ],
        out_specs=pl.BlockSpec((None, LANES, tq), lambda bi, j, qi: (bi, j, qi)),
        out_shape=jax.ShapeDtypeStruct((b, GROUP_WIDTH, s), BF16),
        scratch_shapes=_sweep_scratch(tk, 2 * tq) + [pltpu.VMEM((8, LANES), F32)],
        compiler_params=_cparams("arbitrary", "arbitrary", "arbitrary"),
        name="dilated_attention",
    )(mult, qt, k, vt)


def _with_halo(prev_ref, cur_ref, next_ref, i, n_tiles):
    prev = jnp.where(i > 0, prev_ref[...], 0.0)
    nxt = jnp.where(i < n_tiles - 1, next_ref[...], 0.0)
    return jnp.concatenate([prev, cur_ref[...], nxt], axis=0)


def _shift_rows(x, k):
    n = x.shape[0]
    return pltpu.roll(x, (n - k) % n, 0)


def _pool_kernel(prev_ref, cur_ref, next_ref, w_ref, scale_ref, o_ref, *, seq_len):
    tm = cur_ref.shape[0]
    i = pl.program_id(1)
    x = _with_halo(prev_ref, cur_ref, next_ref, i, seq_len // tm)
    t = i * tm + lax.broadcasted_iota(jnp.int32, (tm, 1), 0)
    lane_group = lax.broadcasted_iota(jnp.int32, (tm, GROUP_WIDTH), 1) // POOL_GROUP

    run = x
    mean = None
    for g, wnd in enumerate(POOL_WINDOWS):
        run = run + _shift_rows(run, wnd // 2)
        total = _shift_rows(run, -(wnd // 2))[HALO:HALO + tm]
        lo = jnp.clip(t - wnd // 2, 0, seq_len - 1)
        hi = jnp.clip(t + wnd - wnd // 2 - 1, 0, seq_len - 1)
        cand = total / (hi - lo + 1).astype(F32)
        mean = cand if mean is None else jnp.where(lane_group == g, cand, mean)
    d = (mean - cur_ref[...]).astype(BF16)
    y = jnp.dot(d, w_ref[...], preferred_element_type=F32) * scale_ref[...]
    o_ref[...] = y.astype(BF16)


def _halo_specs(tm, cols, seq_len):
    nb = tm // HALO
    last = seq_len // HALO - 1
    return [
        pl.BlockSpec((None, HALO, cols), lambda bi, i: (bi, jnp.maximum(i * nb - 1, 0), 0)),
        pl.BlockSpec((None, tm, cols), lambda bi, i: (bi, i, 0)),
        pl.BlockSpec((None, HALO, cols), lambda bi, i: (bi, jnp.minimum((i + 1) * nb, last), 0)),
    ]


def _pool_mixer(hc3, w_bd, scale):
    b, s, c = hc3.shape
    tm = POOL_TM
    prev_s, cur_s, next_s = _halo_specs(tm, c, s)
    return pl.pallas_call(
        functools.partial(_pool_kernel, seq_len=s),
        grid=(b, s // tm),
        in_specs=[prev_s, cur_s, next_s,
                  pl.BlockSpec(w_bd.shape, lambda bi, i: (0, 0)),
                  pl.BlockSpec(scale.shape, lambda bi, i: (0, 0))],
        out_specs=pl.BlockSpec((None, tm, c), lambda bi, i: (bi, i, 0)),
        out_shape=jax.ShapeDtypeStruct((b, s, c), BF16),
        compiler_params=_cparams("arbitrary", "arbitrary"),
        name="pool_mixer",
    )(hc3, hc3, hc3, w_bd, scale)


def _layer_norm(z, g, b):
    mu = jnp.mean(z, axis=-1, keepdims=True)
    zc = z - mu
    var = jnp.mean(zc * zc, axis=-1, keepdims=True)
    return zc * lax.rsqrt(var + LN_EPS) * g + b


def _tdot(at, w):
    return lax.dot_general(at, w, (((0,), (0,)), ((), ())), preferred_element_type=F32)


def _out_kernel(x_ref, ya_ref, yb_ref, yc_ref, yd_ref, w_ref, g_ref, b_ref, o_ref):
    gw = GROUP_WIDTH
    z = _tdot(ya_ref[...], w_ref[0:gw, :])
    z = z + _tdot(yb_ref[...], w_ref[gw:2 * gw, :])
    z = z + jnp.dot(yc_ref[...], w_ref[2 * gw:3 * gw, :], preferred_element_type=F32)
    z = z + _tdot(yd_ref[...], w_ref[3 * gw:4 * gw, :])
    o_ref[...] = _layer_norm(DN_ALPHA * x_ref[...] + z, g_ref[...], b_ref[...])


def _out_proj(x3, ya_t, yb_t, yc, yd_t, w, g, bias):
    b, s, d = x3.shape
    tm = OUT_TM
    tok = lambda n: pl.BlockSpec((None, tm, n), lambda bi, i: (bi, i, 0))
    feat = pl.BlockSpec((None, GROUP_WIDTH, tm), lambda bi, i: (bi, 0, i))
    full = lambda a: pl.BlockSpec(a.shape, lambda bi, i: (0,) * a.ndim)
    return pl.pallas_call(
        _out_kernel,
        grid=(b, s // tm),
        in_specs=[tok(d), feat, feat, tok(GROUP_WIDTH), feat, full(w), full(g), full(bias)],
        out_specs=tok(d),
        out_shape=jax.ShapeDtypeStruct((b, s, d), F32),
        compiler_params=_cparams("arbitrary", "arbitrary"),
        name="out_proj_ln",
    )(x3, ya_t, yb_t, yc, yd_t, w, g, bias)


def _ffn_kernel(prev_ref, cur_ref, next_ref, wup_ref, wd_ref, cw_ref, g_ref, b_ref, o_ref,
                act_ref, *, seq_len):
    tm = cur_ref.shape[0]
    i = pl.program_id(1)
    xe = _with_halo(prev_ref, cur_ref, next_ref, i, seq_len // tm).astype(BF16)

    def conv_proj(j0):
        u = jnp.dot(xe, wup_ref[:, j0:j0 + FFN_CHUNK], preferred_element_type=F32)
        c = cw_ref[:, j0:j0 + FFN_CHUNK]
        y = _shift_rows(u, -1) * c[0:1] + u * c[1:2] + _shift_rows(u, 1) * c[2:3] + c[3:4]
        return y[HALO:HALO + tm]

    for j0 in range(0, D_FF, FFN_CHUNK):
        gate = conv_proj(j0)
        up = conv_proj(D_FF + j0)
        act_ref[:, j0:j0 + FFN_CHUNK] = (gate * jax.nn.sigmoid(gate) * up).astype(BF16)
    f = jnp.dot(act_ref[...], wd_ref[...], preferred_element_type=F32)
    o_ref[...] = _layer_norm(DN_ALPHA * cur_ref[...] + f, g_ref[...], b_ref[...])


def _ffn(x3, wup, wd, cw, g, bias):
    b, s, d = x3.shape
    tm = FFN_TM
    full = lambda a: pl.BlockSpec(a.shape, lambda bi, i: (0,) * a.ndim)
    return pl.pallas_call(
        functools.partial(_ffn_kernel, seq_len=s),
        grid=(b, s // tm),
        in_specs=_halo_specs(tm, d, s) + [full(wup), full(wd), full(cw), full(g), full(bias)],
        out_specs=pl.BlockSpec((None, tm, d), lambda bi, i: (bi, i, 0)),
        out_shape=jax.ShapeDtypeStruct((b, s, d), F32),
        scratch_shapes=[pltpu.VMEM((tm, D_FF), BF16)],
        compiler_params=_cparams("arbitrary", "arbitrary"),
        name="conv_ffn_ln",
    )(x3, x3, x3, wup, wd, cw, g, bias)


def _pad_cols(w, groups, width, padded):
    k = w.shape[0]
    w = w.reshape(k, groups, width)
    return jnp.pad(w, ((0, 0), (0, 0), (0, padded - width))).reshape(k, groups * padded)


def kernel(x, positions, w_in, diff_lambda, diff_subln, pool_w, pool_scale, mla_q_norm,
           mla_kv_norm, mla_w_uq, mla_w_ukv, w_out, ln1_g, ln1_b, ffn_w_up, ffn_conv_w,
           ffn_conv_b, ffn_w_down, ln2_g, ln2_b):
    b, s, d = x.shape
    t = b * s
    assert d == D_MODEL and s % 1024 == 0

    consts = jnp.stack([
        _rope_consts(DIFF_QK_DIM, DIFF_QK_DIM // 4, 0),
        _rope_consts(HEAD_DIM, HEAD_DIM // 4, 0),
        _rope_consts(LANES, MLA_ROPE_DIM, MLA_NOPE_DIM),
    ])
    tables = _rope_tables(positions.reshape(t, 1), consts)
    ta, tb, td = tables[0], tables[1], tables[2]
    mult = _dil_multiplicity(DIL_TQ, DIL_TK)

    for l in range(DEPTH):
        wi = w_in[l]
        w_proj = jnp.concatenate([
            wi[:, :7 * GROUP_WIDTH + MLA_Q_RANK + MLA_KV_RANK],
            jnp.zeros((d, MLA_NOPE_DIM), wi.dtype),
            wi[:, 7 * GROUP_WIDTH + MLA_Q_RANK + MLA_KV_RANK:],
            jnp.zeros((d, LANES - MLA_NOPE_DIM - MLA_ROPE_DIM), wi.dtype)], axis=1).astype(BF16)
        wq = _pad_cols(mla_w_uq[l], 4, MLA_NOPE_DIM + MLA_ROPE_DIM, LANES).astype(BF16)
        wkv = mla_w_ukv[l].reshape(MLA_KV_RANK, 4, 2, HEAD_DIM)
        wk = _pad_cols(wkv[:, :, 0].reshape(MLA_KV_RANK, 4 * HEAD_DIM), 4, HEAD_DIM, LANES).astype(BF16)
        wv = wkv[:, :, 1].reshape(MLA_KV_RANK, 4 * HEAD_DIM).astype(BF16)
        pw = pool_w[l]
        w_pool = jnp.zeros((GROUP_WIDTH, GROUP_WIDTH), pw.dtype)
        for g in range(4):
            w_pool = w_pool.at[g * POOL_GROUP:(g + 1) * POOL_GROUP,
                               g * POOL_GROUP:(g + 1) * POOL_GROUP].set(pw[g])
        w_pool = w_pool.astype(BF16)
        conv = jnp.concatenate([ffn_conv_w[l], ffn_conv_b[l][None, :],
                                jnp.zeros((4, 2 * D_FF), F32)], axis=0)

        qa_t, ka, va_t, qb_t, kb, vb_t, hc, hd = _proj(x, w_proj, ta, tb)
        qd_t, kd, vd_t = _mla_prep(hd, td, mla_q_norm[l][None, :], mla_kv_norm[l][None, :],
                                   wq, wk, wv)
        ya_t = _diff_attention(diff_lambda[l], diff_subln[l][:, None], qa_t, ka, va_t, l)
        yb_t = _dil_attention(mult, qb_t, kb, vb_t)
        yc = _pool_mixer(hc, w_pool, pool_scale[l][None, :])
        yd_t = _mla_attention(qd_t, kd, vd_t)

        x = _out_proj(x, ya_t, yb_t, yc, yd_t, w_out[l].astype(BF16),
                      ln1_g[l][None, :], ln1_b[l][None, :])
        x = _ffn(x, ffn_w_up[l].astype(BF16), ffn_w_down[l].astype(BF16), conv,
                 ln2_g[l][None, :], ln2_b[l][None, :])
    return x
```

```python
import functools
import math

import jax
import jax.numpy as jnp
from jax import lax
from jax.experimental import pallas as pl
from jax.experimental.pallas import tpu as pltpu

F32 = jnp.float32
BF16 = jnp.bfloat16

D_MODEL = 1024
DEPTH = 4
HEAD_DIM = 64
GROUP_WIDTH = 256
ROPE_THETA = 500000.0
NEG_INF = -1e30
LOG2E = math.log2(math.e)

DIFF_QK_DIM = 32
DIL_PATTERNS = ((128, 1), (512, 4), (2048, 16))
POOL_WINDOWS = (2, 4, 8, 16)
POOL_GROUP = 64
MLA_Q_RANK = 256
MLA_KV_RANK = 128
MLA_NOPE_DIM = 64
MLA_ROPE_DIM = 32
D_FF = 2816
DN_ALPHA = (2 * DEPTH) ** 0.25
LN_EPS = 1e-5
RMS_EPS = 1e-6

LANES = 128
SUM_ROWS = 16
VT_ROWS = HEAD_DIM + SUM_ROWS
DIL_HALO = 1024
VMEM_LIMIT = 56 * 1024 * 1024

PROJ_TM = 512
ATT_TK = 512
DIFF_TQ = 256
MLA_TQ = 1024
DIL_TQ = 256
DIL_TK = 256
POOL_TM = 512
OUT_TM = 512
FFN_TM = 512
FFN_CHUNK = 256
HALO = 8


def _cparams(*sem):
    return pltpu.CompilerParams(dimension_semantics=sem, vmem_limit_bytes=VMEM_LIMIT)


def _rope_tables_kernel(pos_ref, c_ref, o_ref):
    pos = pos_ref[...].astype(F32)
    ang = pos * c_ref[0:1, :]
    cs = jnp.cos(ang)
    sn = jnp.sin(ang)
    is_rope = c_ref[1:2, :]
    o_ref[0] = cs * is_rope + (1.0 - is_rope)
    o_ref[1] = -sn * c_ref[2:3, :]
    o_ref[2] = sn * c_ref[3:4, :]


def _rope_consts(period, rot_dim, lane0):
    half = rot_dim // 2
    inv_freq = ROPE_THETA ** (-jnp.arange(half, dtype=F32) * 2.0 / rot_dim)
    lane = jnp.arange(LANES)
    d = lane % period - lane0
    is_rope = (d >= 0) & (d < rot_dim)
    f_idx = jnp.clip(d, 0, rot_dim - 1) % half
    rows = [
        jnp.where(is_rope, inv_freq[f_idx], 0.0),
        is_rope.astype(F32),
        (is_rope & (d < half)).astype(F32),
        (is_rope & (d >= half)).astype(F32),
    ]
    rows += [jnp.zeros((LANES,), F32)] * 4
    return jnp.stack(rows).astype(F32)


def _rope_tables(pos2d, consts):
    t = pos2d.shape[0]
    tm = 1024
    n = consts.shape[0]
    return pl.pallas_call(
        _rope_tables_kernel,
        grid=(n, t // tm),
        in_specs=[
            pl.BlockSpec((tm, 1), lambda j, i: (i, 0)),
            pl.BlockSpec((None, 8, LANES), lambda j, i: (j, 0, 0)),
        ],
        out_specs=pl.BlockSpec((None, 3, tm, LANES), lambda j, i: (j, 0, i, 0)),
        out_shape=jax.ShapeDtypeStruct((n, 3, t, LANES), F32),
        compiler_params=_cparams("arbitrary", "arbitrary"),
        name="rope_tables",
    )(pos2d, consts)


def _rope_apply(h, t_ref, half):
    cos, sa, sb = t_ref[0], t_ref[1], t_ref[2]
    outs = []
    for j in range(h.shape[1] // LANES):
        xj = h[:, LANES * j:LANES * (j + 1)]
        fwd = pltpu.roll(xj, LANES - half, 1)
        bwd = pltpu.roll(xj, half, 1)
        outs.append(xj * cos + fwd * sa + bwd * sb)
    return outs[0] if len(outs) == 1 else jnp.concatenate(outs, axis=1)


PROJ_COLS = 6 * GROUP_WIDTH + GROUP_WIDTH + 512


def _proj_kernel(x_ref, w_ref, ta_ref, tb_ref,
                 qa_ref, ka_ref, va_ref, qb_ref, kb_ref, vb_ref, hc_ref, hd_ref):
    x = x_ref[...].astype(BF16)

    def seg(j0, n):
        return jnp.dot(x, w_ref[:, j0:j0 + n], preferred_element_type=F32)

    g = GROUP_WIDTH
    qa = _rope_apply(seg(0, g), ta_ref, 4) * (LOG2E * DIFF_QK_DIM ** -0.5)
    qa_ref[...] = qa.T.astype(BF16)
    ka_ref[...] = _rope_apply(seg(g, g), ta_ref, 4).astype(BF16)
    _store_values_t(seg(2 * g, g), va_ref)
    qb = _rope_apply(seg(3 * g, g), tb_ref, 8) * (LOG2E * HEAD_DIM ** -0.5)
    qb_ref[...] = qb.T.astype(BF16)
    kb_ref[...] = _rope_apply(seg(4 * g, g), tb_ref, 8).astype(BF16)
    _store_values_t(seg(5 * g, g), vb_ref)
    hc_ref[...] = seg(6 * g, g)
    hd_ref[...] = seg(7 * g, 512)


def _store_values_t(v, vt_ref):
    tk = vt_ref.shape[3]
    vt = v.astype(BF16).T
    ones = jnp.ones((SUM_ROWS, tk), BF16)
    for h in range(4):
        for c in range(vt_ref.shape[1]):
            vt_ref[h, c, 0:HEAD_DIM, :] = vt[HEAD_DIM * h:HEAD_DIM * (h + 1), tk * c:tk * (c + 1)]
            vt_ref[h, c, HEAD_DIM:VT_ROWS, :] = ones


def _token_spec(tm, n):
    return pl.BlockSpec((None, tm, n), lambda bi, i: (bi, i, 0))


def _feature_spec(tm, n):
    return pl.BlockSpec((None, n, tm), lambda bi, i: (bi, 0, i))


def _values_spec(tm, tk):
    return pl.BlockSpec((None, 4, tm // tk, VT_ROWS, tk), lambda bi, i: (bi, 0, i, 0, 0))


def _values_shape(b, s, tk):
    return jax.ShapeDtypeStruct((b, 4, s // tk, VT_ROWS, tk), BF16)


def _table_spec(tm, s):
    return pl.BlockSpec((3, tm, LANES), lambda bi, i: (0, bi * (s // tm) + i, 0))


def _proj(x3, w, ta, tb):
    b, s, d = x3.shape
    tm = PROJ_TM
    g = GROUP_WIDTH
    tok = lambda n, dt: jax.ShapeDtypeStruct((b, s, n), dt)
    feat = jax.ShapeDtypeStruct((b, g, s), BF16)
    return pl.pallas_call(
        _proj_kernel,
        grid=(b, s // tm),
        in_specs=[_token_spec(tm, d), pl.BlockSpec((d, PROJ_COLS), lambda bi, i: (0, 0)),
                  _table_spec(tm, s), _table_spec(tm, s)],
        out_specs=[_feature_spec(tm, g), _token_spec(tm, g), _values_spec(tm, ATT_TK),
                   _feature_spec(tm, g), _token_spec(tm, g), _values_spec(tm, DIL_TK),
                   _token_spec(tm, g), _token_spec(tm, 512)],
        out_shape=[feat, tok(g, BF16), _values_shape(b, s, ATT_TK),
                   feat, tok(g, BF16), _values_shape(b, s, DIL_TK),
                   tok(g, F32), tok(512, F32)],
        compiler_params=_cparams("arbitrary", "arbitrary"),
        name="proj_rope",
    )(x3, w, ta, tb)


def _rms(x, g):
    return x * lax.rsqrt(jnp.mean(x * x, axis=-1, keepdims=True) + RMS_EPS) * g


def _mla_prep_kernel(hd_ref, td_ref, gq_ref, gkv_ref, wq_ref, wk_ref, wv_ref,
                     q_ref, k_ref, v_ref):
    hd = hd_ref[...]
    cq = _rms(hd[:, 0:MLA_Q_RANK], gq_ref[...]).astype(BF16)
    ckv = _rms(hd[:, MLA_Q_RANK:MLA_Q_RANK + MLA_KV_RANK], gkv_ref[...]).astype(BF16)
    q = jnp.dot(cq, wq_ref[...], preferred_element_type=F32)
    scale = LOG2E * (MLA_NOPE_DIM + MLA_ROPE_DIM) ** -0.5
    q_ref[...] = (_rope_apply(q, td_ref, MLA_ROPE_DIM // 2) * scale).T.astype(BF16)
    k_rope = _rope_apply(hd[:, 384:512], td_ref, MLA_ROPE_DIM // 2)
    k = jnp.dot(ckv, wk_ref[...], preferred_element_type=F32)
    k_ref[...] = (k + jnp.concatenate([k_rope] * 4, axis=1)).astype(BF16)
    _store_values_t(jnp.dot(ckv, wv_ref[...], preferred_element_type=F32), v_ref)


def _mla_prep(hd, td, gq, gkv, wq, wk, wv):
    b, s, _ = hd.shape
    tm = PROJ_TM
    full = lambda a: pl.BlockSpec(a.shape, lambda bi, i: (0,) * a.ndim)
    return pl.pallas_call(
        _mla_prep_kernel,
        grid=(b, s // tm),
        in_specs=[_token_spec(tm, 512), _table_spec(tm, s),
                  full(gq), full(gkv), full(wq), full(wk), full(wv)],
        out_specs=[_feature_spec(tm, 512), _token_spec(tm, 512), _values_spec(tm, ATT_TK)],
        out_shape=[jax.ShapeDtypeStruct((b, 512, s), BF16),
                   jax.ShapeDtypeStruct((b, s, 512), BF16), _values_shape(b, s, ATT_TK)],
        compiler_params=_cparams("arbitrary", "arbitrary"),
        name="mla_prep",
    )(hd, td, gq, gkv, wq, wk, wv)


def _flash_sweep(n_chunks, k_chunk, w, v_chunks, widths, s_ref, p_ref, mask_chunk=None):
    n = w.shape[1]
    offs = [sum(widths[:g]) for g in range(len(widths))]
    last = n_chunks - 1

    def scores(c, slot):
        s = jnp.dot(k_chunk(c), w, preferred_element_type=F32)
        if mask_chunk is not None:
            s = jnp.where(mask_chunk(c) > 0.0, s, NEG_INF)
        s_ref[slot] = s
        return jnp.max(s, axis=0, keepdims=True)

    def values(c, slot, accs):
        return tuple(
            acc + jnp.dot(v_chunks[g](c), p_ref[slot, :, offs[g]:offs[g] + widths[g]],
                          preferred_element_type=F32)
            for g, acc in enumerate(accs))

    def step(c, slot, carry):
        m, cmaxes, accs = carry
        m_new = jnp.maximum(m, cmaxes[0])
        alpha = jnp.exp2(m - m_new)
        cmax_new = scores(jnp.minimum(c + SWEEP_AHEAD, last), (slot + SWEEP_AHEAD) % SWEEP_SLOTS)
        accs = values(jnp.maximum(c - 1, 0), (slot + 1) % 2, accs)
        accs = tuple(acc * alpha[:, offs[g]:offs[g] + widths[g]] for g, acc in enumerate(accs))
        p = jnp.exp2(s_ref[slot] - m_new)
        if mask_chunk is not None:
            p = p * mask_chunk(c)
        p_ref[slot % 2] = p.astype(BF16)
        return m_new, cmaxes[1:] + (cmax_new,), accs

    p_ref[1] = jnp.zeros(p_ref.shape[1:], BF16)
    carry = (jnp.full((1, n), NEG_INF, F32),
             tuple(scores(min(i, last), i) for i in range(SWEEP_AHEAD)),
             tuple(jnp.zeros((VT_ROWS, wd), F32) for wd in widths))

    def trip(j, carry):
        for i in range(SWEEP_STEPS):
            carry = step(SWEEP_STEPS * j + i, i % SWEEP_SLOTS, carry)
        return carry

    looped = n_chunks // SWEEP_STEPS * SWEEP_STEPS
    carry = lax.fori_loop(0, n_chunks // SWEEP_STEPS, trip, carry)
    for c in range(looped, n_chunks):
        carry = step(c, c % SWEEP_SLOTS, carry)
    return values(last, last % 2, carry[2])


SWEEP_SLOTS = 4
SWEEP_AHEAD = 2
SWEEP_STEPS = 8


def _sweep_scratch(tk, n):
    return [pltpu.VMEM((SWEEP_SLOTS, tk, n), F32), pltpu.VMEM((P_SLOTS, tk, n), BF16)]


SAFE_LOG2_SPAN = 100.0
KMAX_ROWS = 1024


def _key_absmax(k_ref, kmax_ref, refresh):
    @pl.when(refresh)
    def _():
        def body(i, m):
            blk = k_ref[pl.ds(pl.multiple_of(i * KMAX_ROWS, KMAX_ROWS), KMAX_ROWS), :]
            return jnp.maximum(m, jnp.max(jnp.abs(blk.astype(F32)), axis=0, keepdims=True))
        m = lax.fori_loop(0, k_ref.shape[0] // KMAX_ROWS, body, jnp.zeros((1, LANES), F32))
        kmax_ref[...] = jnp.broadcast_to(m, kmax_ref.shape)


def _fixed_sweep(n_chunks, k_chunk, w, v_chunks, widths, s_ref, p_ref, m_ref, mask_chunk, first):
    offs = [sum(widths[:g]) for g in range(len(widths))]

    def chunk_of(t):
        return first if t == 0 else (t if t > first else t - 1)

    def probs(s, t):
        p = jnp.exp2(s - m_ref)
        if mask_chunk is not None:
            p = p * mask_chunk(chunk_of(t))
        p_ref[t % P_SLOTS] = p.astype(BF16)

    def values(t, accs):
        c = chunk_of(t)
        return tuple(
            acc + jnp.dot(v_chunks[g](c), p_ref[t % P_SLOTS, :, offs[g]:offs[g] + widths[g]],
                          preferred_element_type=F32)
            for g, acc in enumerate(accs))

    probs(s_ref[0], 0)
    accs = tuple(jnp.zeros((VT_ROWS, wd), F32) for wd in widths)

    for t in range(1, n_chunks):
        s = jnp.dot(k_chunk(chunk_of(t)), w, preferred_element_type=F32)
        if t >= PV_LAG:
            accs = values(t - PV_LAG, accs)
        probs(s, t)
    for t in range(max(n_chunks - PV_LAG, 0), n_chunks):
        accs = values(t, accs)
    return accs


PV_LAG = 3
P_SLOTS = PV_LAG + 1


def _softmax_sweep(n_chunks, k_chunk, w, v_chunks, widths, s_ref, p_ref, kmax_ref,
                   mask_chunk=None, first=0):
    s0 = jnp.dot(k_chunk(first), w, preferred_element_type=F32)
    s_ref[0] = s0
    if mask_chunk is not None:
        s0 = jnp.where(mask_chunk(first) > 0.0, s0, NEG_INF)
    m0 = jnp.max(s0, axis=0, keepdims=True)
    kmax = (kmax_ref[...] * 1.01).astype(BF16)
    bound = jnp.dot(kmax, jnp.abs(w), preferred_element_type=F32)[0:1]
    safe = jnp.max(bound - m0) <= SAFE_LOG2_SPAN
    return lax.cond(
        safe,
        lambda: _fixed_sweep(n_chunks, k_chunk, w, v_chunks, widths, s_ref, p_ref, m0,
                             mask_chunk, first),
        lambda: _flash_sweep(n_chunks, k_chunk, w, v_chunks, widths, s_ref, p_ref, mask_chunk))


def _diff_kernel(lam_ref, g_ref, qt_ref, k_ref, vt_ref, o_ref, s_ref, p_ref, kmax_ref, *,
                 lam_init, n_chunks):
    tq = qt_ref.shape[1]
    tk = vt_ref.shape[3]
    lp = lam_ref[...]
    lam = (jnp.exp(jnp.sum(lp[0:1] * lp[1:2], axis=1, keepdims=True))
           - jnp.exp(jnp.sum(lp[2:3] * lp[3:4], axis=1, keepdims=True)) + lam_init)

    qt = qt_ref[...].astype(F32)
    row = lax.broadcasted_iota(jnp.int32, (LANES, tq), 0)
    cols = [jnp.where((row >= DIFF_QK_DIM * c) & (row < DIFF_QK_DIM * (c + 1)), qt, 0.0)
            for c in range(4)]
    w = jnp.concatenate(cols, axis=1).astype(BF16)

    _key_absmax(k_ref, kmax_ref, pl.program_id(2) == 0)
    accs = _softmax_sweep(
        n_chunks,
        lambda c: k_ref[pl.ds(pl.multiple_of(c * tk, tk), tk), :],
        w,
        [lambda c: vt_ref[0, c], lambda c: vt_ref[1, c]],
        [2 * tq, 2 * tq], s_ref, p_ref, kmax_ref)

    gain = g_ref[...] * (1.0 - lam_init)
    for hl, acc in enumerate(accs):
        o1 = acc[0:HEAD_DIM, 0:tq] / acc[HEAD_DIM:HEAD_DIM + 1, 0:tq]
        o2 = acc[0:HEAD_DIM, tq:2 * tq] / acc[HEAD_DIM:HEAD_DIM + 1, tq:2 * tq]
        o = o1 - lam * o2
        ms = jnp.mean(o * o, axis=0, keepdims=True)
        y = o * lax.rsqrt(ms + RMS_EPS) * gain
        o_ref[HEAD_DIM * hl:HEAD_DIM * (hl + 1), :] = y.astype(BF16)


def _diff_attention(lam_params, subln_g, qt, k, vt, layer_idx):
    b, _, s = qt.shape
    tq, tk = DIFF_TQ, ATT_TK
    nck = s // tk
    lam_init = 0.8 - 0.6 * math.exp(-0.3 * layer_idx)
    kern = functools.partial(_diff_kernel, lam_init=lam_init, n_chunks=nck)
    return pl.pallas_call(
        kern,
        grid=(b, 2, s // tq),
        in_specs=[
            pl.BlockSpec((4, DIFF_QK_DIM), lambda bi, j, qi: (0, 0)),
            pl.BlockSpec((HEAD_DIM, 1), lambda bi, j, qi: (0, 0)),
            pl.BlockSpec((None, LANES, tq), lambda bi, j, qi: (bi, j, qi)),
            pl.BlockSpec((None, s, LANES), lambda bi, j, qi: (bi, 0, j)),
            pl.BlockSpec((None, 2, nck, VT_ROWS, tk), lambda bi, j, qi: (bi, j, 0, 0, 0)),
        ],
        out_specs=pl.BlockSpec((None, LANES, tq), lambda bi, j, qi: (bi, j, qi)),
        out_shape=jax.ShapeDtypeStruct((b, GROUP_WIDTH, s), BF16),
        scratch_shapes=_sweep_scratch(tk, 4 * tq) + [pltpu.VMEM((8, LANES), F32)],
        compiler_params=_cparams("arbitrary", "arbitrary", "arbitrary"),
        name="diff_attention",
    )(lam_params, subln_g, qt, k, vt)


def _mla_kernel(qt_ref, k_ref, vt_ref, o_ref, s_ref, p_ref, kmax_ref, *, n_chunks):
    tq = qt_ref.shape[1]
    tk = vt_ref.shape[2]
    _key_absmax(k_ref, kmax_ref, pl.program_id(2) == 0)
    (acc,) = _softmax_sweep(
        n_chunks,
        lambda c: k_ref[pl.ds(pl.multiple_of(c * tk, tk), tk), :],
        qt_ref[...],
        [lambda c: vt_ref[c]],
        [tq], s_ref, p_ref, kmax_ref)
    o_ref[...] = (acc[0:HEAD_DIM] / acc[HEAD_DIM:HEAD_DIM + 1]).astype(BF16)


def _mla_attention(qt, k, vt):
    b, _, s = qt.shape
    tq, tk = MLA_TQ, ATT_TK
    nck = s // tk
    return pl.pallas_call(
        functools.partial(_mla_kernel, n_chunks=nck),
        grid=(b, 4, s // tq),
        in_specs=[
            pl.BlockSpec((None, LANES, tq), lambda bi, h, qi: (bi, h, qi)),
            pl.BlockSpec((None, s, LANES), lambda bi, h, qi: (bi, 0, h)),
            pl.BlockSpec((None, None, nck, VT_ROWS, tk), lambda bi, h, qi: (bi, h, 0, 0, 0)),
        ],
        out_specs=pl.BlockSpec((None, HEAD_DIM, tq), lambda bi, h, qi: (bi, h, qi)),
        out_shape=jax.ShapeDtypeStruct((b, GROUP_WIDTH, s), BF16),
        scratch_shapes=_sweep_scratch(tk, tq) + [pltpu.VMEM((8, LANES), F32)],
        compiler_params=_cparams("arbitrary", "arbitrary", "arbitrary"),
        name="mla_attention",
    )(qt, k, vt)


def _dil_multiplicity(tq, tk):
    n_win = (tq + 2 * DIL_HALO) // tk
    key = jnp.arange(n_win * tk)[:, None] - DIL_HALO
    delta = key - jnp.arange(tq)[None, :]
    mult = jnp.zeros(delta.shape, F32)
    for window, dil in DIL_PATTERNS:
        reach = (window // 2 // dil) * dil
        mult = mult + ((delta % dil == 0) & (jnp.abs(delta) <= reach)).astype(F32)
    return mult.reshape(n_win, tk, tq)


def _dil_kernel(c_ref, qt_ref, k_ref, vt_ref, o_ref, s_ref, p_ref, kmax_ref, *, n_seq_chunks):
    tq = qt_ref.shape[1]
    n_win, tk = c_ref.shape[0], c_ref.shape[1]
    halo_chunks = DIL_HALO // tk
    qi = pl.program_id(2)

    qt = qt_ref[...].astype(F32)
    row = lax.broadcasted_iota(jnp.int32, (LANES, tq), 0)
    w = jnp.concatenate([jnp.where(row < HEAD_DIM, qt, 0.0),
                         jnp.where(row >= HEAD_DIM, qt, 0.0)], axis=1).astype(BF16)

    def seq_chunk(c):
        return qi * (tq // tk) + c - halo_chunks

    def clamped(c):
        return jnp.clip(seq_chunk(c), 0, n_seq_chunks - 1)

    def mask_chunk(c):
        ck = seq_chunk(c)
        inside = jnp.where((ck >= 0) & (ck < n_seq_chunks), 1.0, 0.0)
        mult = c_ref[c] * inside
        return jnp.concatenate([mult, mult], axis=1)

    def k_chunk(c):
        return k_ref[pl.ds(pl.multiple_of(clamped(c) * tk, tk), tk), :]

    _key_absmax(k_ref, kmax_ref, qi == 0)
    accs = _softmax_sweep(
        n_win, k_chunk, w,
        [lambda c: vt_ref[0, clamped(c)], lambda c: vt_ref[1, clamped(c)]],
        [tq, tq], s_ref, p_ref, kmax_ref, mask_chunk=mask_chunk, first=n_win // 2)
    for hl, acc in enumerate(accs):
        o_ref[HEAD_DIM * hl:HEAD_DIM * (hl + 1), :] = (
            acc[0:HEAD_DIM] / acc[HEAD_DIM:HEAD_DIM + 1]).astype(BF16)


def _dil_attention(mult, qt, k, vt):
    b, _, s = qt.shape
    tq, tk = DIL_TQ, DIL_TK
    n_win = mult.shape[0]
    return pl.pallas_call(
        functools.partial(_dil_kernel, n_seq_chunks=s // tk),
        grid=(b, 2, s // tq),
        in_specs=[
            pl.BlockSpec((n_win, tk, tq), lambda bi, j, qi: (0, 0, 0)),
            pl.BlockSpec((None, LANES, tq), lambda bi, j, qi: (bi, j, qi)),
            pl.BlockSpec((None, s, LANES), lambda bi, j, qi: (bi, 0, j)),
            pl.BlockSpec((None, 2, s // tk, VT_ROWS, tk), lambda bi, j, qi: (bi, j, 0, 0, 0)),
        ],
        out_specs=pl.BlockSpec((None, LANES, tq), lambda bi, j, qi: (bi, j, qi)),
        out_shape=jax.ShapeDtypeStruct((b, GROUP_WIDTH, s), BF16),
        scratch_shapes=_sweep_scratch(tk, 2 * tq) + [pltpu.VMEM((8, LANES), F32)],
        compiler_params=_cparams("arbitrary", "arbitrary", "arbitrary"),
        name="dilated_attention",
    )(mult, qt, k, vt)


def _with_halo(prev_ref, cur_ref, next_ref, i, n_tiles):
    prev = jnp.where(i > 0, prev_ref[...], 0.0)
    nxt = jnp.where(i < n_tiles - 1, next_ref[...], 0.0)
    return jnp.concatenate([prev, cur_ref[...], nxt], axis=0)


def _shift_rows(x, k):
    n = x.shape[0]
    return pltpu.roll(x, (n - k) % n, 0)


def _pool_kernel(prev_ref, cur_ref, next_ref, w_ref, scale_ref, o_ref, *, seq_len):
    tm = cur_ref.shape[0]
    i = pl.program_id(1)
    x = _with_halo(prev_ref, cur_ref, next_ref, i, seq_len // tm)
    t = i * tm + lax.broadcasted_iota(jnp.int32, (tm, 1), 0)
    lane_group = lax.broadcasted_iota(jnp.int32, (tm, GROUP_WIDTH), 1) // POOL_GROUP

    run = x
    mean = None
    for g, wnd in enumerate(POOL_WINDOWS):
        run = run + _shift_rows(run, wnd // 2)
        total = _shift_rows(run, -(wnd // 2))[HALO:HALO + tm]
        lo = jnp.clip(t - wnd // 2, 0, seq_len - 1)
        hi = jnp.clip(t + wnd - wnd // 2 - 1, 0, seq_len - 1)
        cand = total / (hi - lo + 1).astype(F32)
        mean = cand if mean is None else jnp.where(lane_group == g, cand, mean)
    d = (mean - cur_ref[...]).astype(BF16)
    y = jnp.dot(d, w_ref[...], preferred_element_type=F32) * scale_ref[...]
    o_ref[...] = y.astype(BF16)


def _halo_specs(tm, cols, seq_len):
    nb = tm // HALO
    last = seq_len // HALO - 1
    return [
        pl.BlockSpec((None, HALO, cols), lambda bi, i: (bi, jnp.maximum(i * nb - 1, 0), 0)),
        pl.BlockSpec((None, tm, cols), lambda bi, i: (bi, i, 0)),
        pl.BlockSpec((None, HALO, cols), lambda bi, i: (bi, jnp.minimum((i + 1) * nb, last), 0)),
    ]


def _pool_mixer(hc3, w_bd, scale):
    b, s, c = hc3.shape
    tm = POOL_TM
    prev_s, cur_s, next_s = _halo_specs(tm, c, s)
    return pl.pallas_call(
        functools.partial(_pool_kernel, seq_len=s),
        grid=(b, s // tm),
        in_specs=[prev_s, cur_s, next_s,
                  pl.BlockSpec(w_bd.shape, lambda bi, i: (0, 0)),
                  pl.BlockSpec(scale.shape, lambda bi, i: (0, 0))],
        out_specs=pl.BlockSpec((None, tm, c), lambda bi, i: (bi, i, 0)),
        out_shape=jax.ShapeDtypeStruct((b, s, c), BF16),
        compiler_params=_cparams("arbitrary", "arbitrary"),
        name="pool_mixer",
    )(hc3, hc3, hc3, w_bd, scale)


def _layer_norm(z, g, b):
    mu = jnp.mean(z, axis=-1, keepdims=True)
    zc = z - mu
    var = jnp.mean(zc * zc, axis=-1, keepdims=True)
    return zc * lax.rsqrt(var + LN_EPS) * g + b


def _tdot(at, w):
    return lax.dot_general(at, w, (((0,), (0,)), ((), ())), preferred_element_type=F32)


def _out_kernel(x_ref, ya_ref, yb_ref, yc_ref, yd_ref, w_ref, g_ref, b_ref, o_ref):
    gw = GROUP_WIDTH
    z = _tdot(ya_ref[...], w_ref[0:gw, :])
    z = z + _tdot(yb_ref[...], w_ref[gw:2 * gw, :])
    z = z + jnp.dot(yc_ref[...], w_ref[2 * gw:3 * gw, :], preferred_element_type=F32)
    z = z + _tdot(yd_ref[...], w_ref[3 * gw:4 * gw, :])
    o_ref[...] = _layer_norm(DN_ALPHA * x_ref[...] + z, g_ref[...], b_ref[...])


def _out_proj(x3, ya_t, yb_t, yc, yd_t, w, g, bias):
    b, s, d = x3.shape
    tm = OUT_TM
    tok = lambda n: pl.BlockSpec((None, tm, n), lambda bi, i: (bi, i, 0))
    feat = pl.BlockSpec((None, GROUP_WIDTH, tm), lambda bi, i: (bi, 0, i))
    full = lambda a: pl.BlockSpec(a.shape, lambda bi, i: (0,) * a.ndim)
    return pl.pallas_call(
        _out_kernel,
        grid=(b, s // tm),
        in_specs=[tok(d), feat, feat, tok(GROUP_WIDTH), feat, full(w), full(g), full(bias)],
        out_specs=tok(d),
        out_shape=jax.ShapeDtypeStruct((b, s, d), F32),
        compiler_params=_cparams("arbitrary", "arbitrary"),
        name="out_proj_ln",
    )(x3, ya_t, yb_t, yc, yd_t, w, g, bias)


def _ffn_kernel(prev_ref, cur_ref, next_ref, wup_ref, wd_ref, cw_ref, g_ref, b_ref, o_ref,
                act_ref, *, seq_len):
    tm = cur_ref.shape[0]
    i = pl.program_id(1)
    xe = _with_halo(prev_ref, cur_ref, next_ref, i, seq_len // tm).astype(BF16)

    def conv_proj(j0):
        u = jnp.dot(xe, wup_ref[:, j0:j0 + FFN_CHUNK], preferred_element_type=F32)
        c = cw_ref[:, j0:j0 + FFN_CHUNK]
        y = _shift_rows(u, -1) * c[0:1] + u * c[1:2] + _shift_rows(u, 1) * c[2:3] + c[3:4]
        return y[HALO:HALO + tm]

    for j0 in range(0, D_FF, FFN_CHUNK):
        gate = conv_proj(j0)
        up = conv_proj(D_FF + j0)
        act_ref[:, j0:j0 + FFN_CHUNK] = (gate * jax.nn.sigmoid(gate) * up).astype(BF16)
    f = jnp.dot(act_ref[...], wd_ref[...], preferred_element_type=F32)
    o_ref[...] = _layer_norm(DN_ALPHA * cur_ref[...] + f, g_ref[...], b_ref[...])


def _ffn(x3, wup, wd, cw, g, bias):
    b, s, d = x3.shape
    tm = FFN_TM
    full = lambda a: pl.BlockSpec(a.shape, lambda bi, i: (0,) * a.ndim)
    return pl.pallas_call(
        functools.partial(_ffn_kernel, seq_len=s),
        grid=(b, s // tm),
        in_specs=_halo_specs(tm, d, s) + [full(wup), full(wd), full(cw), full(g), full(bias)],
        out_specs=pl.BlockSpec((None, tm, d), lambda bi, i: (bi, i, 0)),
        out_shape=jax.ShapeDtypeStruct((b, s, d), F32),
        scratch_shapes=[pltpu.VMEM((tm, D_FF), BF16)],
        compiler_params=_cparams("arbitrary", "arbitrary"),
        name="conv_ffn_ln",
    )(x3, x3, x3, wup, wd, cw, g, bias)


def _pad_cols(w, groups, width, padded):
    k = w.shape[0]
    w = w.reshape(k, groups, width)
    return jnp.pad(w, ((0, 0), (0, 0), (0, padded - width))).reshape(k, groups * padded)


def kernel(x, positions, w_in, diff_lambda, diff_subln, pool_w, pool_scale, mla_q_norm,
           mla_kv_norm, mla_w_uq, mla_w_ukv, w_out, ln1_g, ln1_b, ffn_w_up, ffn_conv_w,
           ffn_conv_b, ffn_w_down, ln2_g, ln2_b):
    b, s, d = x.shape
    t = b * s
    assert d == D_MODEL and s % 1024 == 0

    consts = jnp.stack([
        _rope_consts(DIFF_QK_DIM, DIFF_QK_DIM // 4, 0),
        _rope_consts(HEAD_DIM, HEAD_DIM // 4, 0),
        _rope_consts(LANES, MLA_ROPE_DIM, MLA_NOPE_DIM),
    ])
    tables = _rope_tables(positions.reshape(t, 1), consts)
    ta, tb, td = tables[0], tables[1], tables[2]
    mult = _dil_multiplicity(DIL_TQ, DIL_TK)

    for l in range(DEPTH):
        wi = w_in[l]
        w_proj = jnp.concatenate([
            wi[:, :7 * GROUP_WIDTH + MLA_Q_RANK + MLA_KV_RANK],
            jnp.zeros((d, MLA_NOPE_DIM), wi.dtype),
            wi[:, 7 * GROUP_WIDTH + MLA_Q_RANK + MLA_KV_RANK:],
            jnp.zeros((d, LANES - MLA_NOPE_DIM - MLA_ROPE_DIM), wi.dtype)], axis=1).astype(BF16)
        wq = _pad_cols(mla_w_uq[l], 4, MLA_NOPE_DIM + MLA_ROPE_DIM, LANES).astype(BF16)
        wkv = mla_w_ukv[l].reshape(MLA_KV_RANK, 4, 2, HEAD_DIM)
        wk = _pad_cols(wkv[:, :, 0].reshape(MLA_KV_RANK, 4 * HEAD_DIM), 4, HEAD_DIM, LANES).astype(BF16)
        wv = wkv[:, :, 1].reshape(MLA_KV_RANK, 4 * HEAD_DIM).astype(BF16)
        pw = pool_w[l]
        w_pool = jnp.zeros((GROUP_WIDTH, GROUP_WIDTH), pw.dtype)
        for g in range(4):
            w_pool = w_pool.at[g * POOL_GROUP:(g + 1) * POOL_GROUP,
                               g * POOL_GROUP:(g + 1) * POOL_GROUP].set(pw[g])
        w_pool = w_pool.astype(BF16)
        conv = jnp.concatenate([ffn_conv_w[l], ffn_conv_b[l][None, :],
                                jnp.zeros((4, 2 * D_FF), F32)], axis=0)

        qa_t, ka, va_t, qb_t, kb, vb_t, hc, hd = _proj(x, w_proj, ta, tb)
        qd_t, kd, vd_t = _mla_prep(hd, td, mla_q_norm[l][None, :], mla_kv_norm[l][None, :],
                                   wq, wk, wv)
        ya_t = _diff_attention(diff_lambda[l], diff_subln[l][:, None], qa_t, ka, va_t, l)
        yb_t = _dil_attention(mult, qb_t, kb, vb_t)
        yc = _pool_mixer(hc, w_pool, pool_scale[l][None, :])
        yd_t = _mla_attention(qd_t, kd, vd_t)

        x = _out_proj(x, ya_t, yb_t, yc, yd_t, w_out[l].astype(BF16),
                      ln1_g[l][None, :], ln1_b[l][None, :])
        x = _ffn(x, ffn_w_up[l].astype(BF16), ffn_w_down[l].astype(BF16), conv,
                 ln2_g[l][None, :], ln2_b[l][None, :])
    return x
```

```python
import functools
import math

import jax
import jax.numpy as jnp
from jax import lax
from jax.experimental import pallas as pl
from jax.experimental.pallas import tpu as pltpu

F32 = jnp.float32
BF16 = jnp.bfloat16

D_MODEL = 1024
DEPTH = 4
HEAD_DIM = 64
GROUP_WIDTH = 256
ROPE_THETA = 500000.0
NEG_INF = -1e30
LOG2E = math.log2(math.e)

DIFF_QK_DIM = 32
DIL_PATTERNS = ((128, 1), (512, 4), (2048, 16))
POOL_WINDOWS = (2, 4, 8, 16)
POOL_GROUP = 64
MLA_Q_RANK = 256
MLA_KV_RANK = 128
MLA_NOPE_DIM = 64
MLA_ROPE_DIM = 32
D_FF = 2816
DN_ALPHA = (2 * DEPTH) ** 0.25
LN_EPS = 1e-5
RMS_EPS = 1e-6

LANES = 128
SUM_ROWS = 16
VT_ROWS = HEAD_DIM + SUM_ROWS
DIL_HALO = 1024
VMEM_LIMIT = 56 * 1024 * 1024

PROJ_TM = 512
ATT_TK = 512
DIFF_TQ = 512
MLA_TQ = 1024
DIL_TQ = 256
DIL_TK = 256
POOL_TM = 512
OUT_TM = 512
FFN_TM = 512
FFN_CHUNK = 256
HALO = 8


def _cparams(*sem):
    return pltpu.CompilerParams(dimension_semantics=sem, vmem_limit_bytes=VMEM_LIMIT)


def _rope_tables_kernel(pos_ref, c_ref, o_ref):
    pos = pos_ref[...].astype(F32)
    ang = pos * c_ref[0:1, :]
    cs = jnp.cos(ang)
    sn = jnp.sin(ang)
    is_rope = c_ref[1:2, :]
    o_ref[0] = cs * is_rope + (1.0 - is_rope)
    o_ref[1] = -sn * c_ref[2:3, :]
    o_ref[2] = sn * c_ref[3:4, :]


def _rope_consts(period, rot_dim, lane0):
    half = rot_dim // 2
    inv_freq = ROPE_THETA ** (-jnp.arange(half, dtype=F32) * 2.0 / rot_dim)
    lane = jnp.arange(LANES)
    d = lane % period - lane0
    is_rope = (d >= 0) & (d < rot_dim)
    f_idx = jnp.clip(d, 0, rot_dim - 1) % half
    rows = [
        jnp.where(is_rope, inv_freq[f_idx], 0.0),
        is_rope.astype(F32),
        (is_rope & (d < half)).astype(F32),
        (is_rope & (d >= half)).astype(F32),
    ]
    rows += [jnp.zeros((LANES,), F32)] * 4
    return jnp.stack(rows).astype(F32)


def _rope_tables(pos2d, consts):
    t = pos2d.shape[0]
    tm = 1024
    n = consts.shape[0]
    return pl.pallas_call(
        _rope_tables_kernel,
        grid=(n, t // tm),
        in_specs=[
            pl.BlockSpec((tm, 1), lambda j, i: (i, 0)),
            pl.BlockSpec((None, 8, LANES), lambda j, i: (j, 0, 0)),
        ],
        out_specs=pl.BlockSpec((None, 3, tm, LANES), lambda j, i: (j, 0, i, 0)),
        out_shape=jax.ShapeDtypeStruct((n, 3, t, LANES), F32),
        compiler_params=_cparams("arbitrary", "arbitrary"),
        name="rope_tables",
    )(pos2d, consts)


def _rope_apply(h, t_ref, half):
    cos, sa, sb = t_ref[0], t_ref[1], t_ref[2]
    outs = []
    for j in range(h.shape[1] // LANES):
        xj = h[:, LANES * j:LANES * (j + 1)]
        fwd = pltpu.roll(xj, LANES - half, 1)
        bwd = pltpu.roll(xj, half, 1)
        outs.append(xj * cos + fwd * sa + bwd * sb)
    return outs[0] if len(outs) == 1 else jnp.concatenate(outs, axis=1)


PROJ_COLS = 6 * GROUP_WIDTH + GROUP_WIDTH + 512


def _proj_kernel(x_ref, w_ref, ta_ref, tb_ref,
                 qa_ref, ka_ref, va_ref, qb_ref, kb_ref, vb_ref, hc_ref, hd_ref):
    x = x_ref[...].astype(BF16)

    def seg(j0, n):
        return jnp.dot(x, w_ref[:, j0:j0 + n], preferred_element_type=F32)

    g = GROUP_WIDTH
    qa = _rope_apply(seg(0, g), ta_ref, 4) * (LOG2E * DIFF_QK_DIM ** -0.5)
    qa_ref[...] = qa.T.astype(BF16)
    ka_ref[...] = _rope_apply(seg(g, g), ta_ref, 4).astype(BF16)
    _store_values_t(seg(2 * g, g), va_ref)
    qb = _rope_apply(seg(3 * g, g), tb_ref, 8) * (LOG2E * HEAD_DIM ** -0.5)
    qb_ref[...] = qb.T.astype(BF16)
    kb_ref[...] = _rope_apply(seg(4 * g, g), tb_ref, 8).astype(BF16)
    _store_values_t(seg(5 * g, g), vb_ref)
    hc_ref[...] = seg(6 * g, g)
    hd_ref[...] = seg(7 * g, 512)


def _store_values_t(v, vt_ref):
    tk = vt_ref.shape[3]
    vt = v.astype(BF16).T
    ones = jnp.ones((SUM_ROWS, tk), BF16)
    for h in range(4):
        for c in range(vt_ref.shape[1]):
            vt_ref[h, c, 0:HEAD_DIM, :] = vt[HEAD_DIM * h:HEAD_DIM * (h + 1), tk * c:tk * (c + 1)]
            vt_ref[h, c, HEAD_DIM:VT_ROWS, :] = ones


def _token_spec(tm, n):
    return pl.BlockSpec((None, tm, n), lambda bi, i: (bi, i, 0))


def _feature_spec(tm, n):
    return pl.BlockSpec((None, n, tm), lambda bi, i: (bi, 0, i))


def _values_spec(tm, tk):
    return pl.BlockSpec((None, 4, tm // tk, VT_ROWS, tk), lambda bi, i: (bi, 0, i, 0, 0))


def _values_shape(b, s, tk):
    return jax.ShapeDtypeStruct((b, 4, s // tk, VT_ROWS, tk), BF16)


def _table_spec(tm, s):
    return pl.BlockSpec((3, tm, LANES), lambda bi, i: (0, bi * (s // tm) + i, 0))


def _proj(x3, w, ta, tb):
    b, s, d = x3.shape
    tm = PROJ_TM
    g = GROUP_WIDTH
    tok = lambda n, dt: jax.ShapeDtypeStruct((b, s, n), dt)
    feat = jax.ShapeDtypeStruct((b, g, s), BF16)
    return pl.pallas_call(
        _proj_kernel,
        grid=(b, s // tm),
        in_specs=[_token_spec(tm, d), pl.BlockSpec((d, PROJ_COLS), lambda bi, i: (0, 0)),
                  _table_spec(tm, s), _table_spec(tm, s)],
        out_specs=[_feature_spec(tm, g), _token_spec(tm, g), _values_spec(tm, ATT_TK),
                   _feature_spec(tm, g), _token_spec(tm, g), _values_spec(tm, DIL_TK),
                   _token_spec(tm, g), _token_spec(tm, 512)],
        out_shape=[feat, tok(g, BF16), _values_shape(b, s, ATT_TK),
                   feat, tok(g, BF16), _values_shape(b, s, DIL_TK),
                   tok(g, F32), tok(512, F32)],
        compiler_params=_cparams("arbitrary", "arbitrary"),
        name="proj_rope",
    )(x3, w, ta, tb)


def _rms(x, g):
    return x * lax.rsqrt(jnp.mean(x * x, axis=-1, keepdims=True) + RMS_EPS) * g


def _mla_prep_kernel(hd_ref, td_ref, gq_ref, gkv_ref, wq_ref, wk_ref, wv_ref,
                     q_ref, k_ref, v_ref):
    hd = hd_ref[...]
    cq = _rms(hd[:, 0:MLA_Q_RANK], gq_ref[...]).astype(BF16)
    ckv = _rms(hd[:, MLA_Q_RANK:MLA_Q_RANK + MLA_KV_RANK], gkv_ref[...]).astype(BF16)
    q = jnp.dot(cq, wq_ref[...], preferred_element_type=F32)
    scale = LOG2E * (MLA_NOPE_DIM + MLA_ROPE_DIM) ** -0.5
    q_ref[...] = (_rope_apply(q, td_ref, MLA_ROPE_DIM // 2) * scale).T.astype(BF16)
    k_rope = _rope_apply(hd[:, 384:512], td_ref, MLA_ROPE_DIM // 2)
    k = jnp.dot(ckv, wk_ref[...], preferred_element_type=F32)
    k_ref[...] = (k + jnp.concatenate([k_rope] * 4, axis=1)).astype(BF16)
    _store_values_t(jnp.dot(ckv, wv_ref[...], preferred_element_type=F32), v_ref)


def _mla_prep(hd, td, gq, gkv, wq, wk, wv):
    b, s, _ = hd.shape
    tm = PROJ_TM
    full = lambda a: pl.BlockSpec(a.shape, lambda bi, i: (0,) * a.ndim)
    return pl.pallas_call(
        _mla_prep_kernel,
        grid=(b, s // tm),
        in_specs=[_token_spec(tm, 512), _table_spec(tm, s),
                  full(gq), full(gkv), full(wq), full(wk), full(wv)],
        out_specs=[_feature_spec(tm, 512), _token_spec(tm, 512), _values_spec(tm, ATT_TK)],
        out_shape=[jax.ShapeDtypeStruct((b, 512, s), BF16),
                   jax.ShapeDtypeStruct((b, s, 512), BF16), _values_shape(b, s, ATT_TK)],
        compiler_params=_cparams("arbitrary", "arbitrary"),
        name="mla_prep",
    )(hd, td, gq, gkv, wq, wk, wv)


def _flash_sweep(n_chunks, k_chunk, w, v_chunks, widths, s_ref, p_ref, mask_chunk=None):
    n = w.shape[1]
    offs = [sum(widths[:g]) for g in range(len(widths))]
    last = n_chunks - 1

    def scores(c, slot):
        s = jnp.dot(k_chunk(c), w, preferred_element_type=F32)
        if mask_chunk is not None:
            s = jnp.where(mask_chunk(c) > 0.0, s, NEG_INF)
        s_ref[slot] = s
        return jnp.max(s, axis=0, keepdims=True)

    def values(c, slot, accs):
        return tuple(
            acc + jnp.dot(v_chunks[g](c), p_ref[slot, :, offs[g]:offs[g] + widths[g]],
                          preferred_element_type=F32)
            for g, acc in enumerate(accs))

    def step(c, slot, carry):
        m, cmaxes, accs = carry
        m_new = jnp.maximum(m, cmaxes[0])
        alpha = jnp.exp2(m - m_new)
        cmax_new = scores(jnp.minimum(c + SWEEP_AHEAD, last), (slot + SWEEP_AHEAD) % SWEEP_SLOTS)
        accs = values(jnp.maximum(c - 1, 0), (slot + 1) % 2, accs)
        accs = tuple(acc * alpha[:, offs[g]:offs[g] + widths[g]] for g, acc in enumerate(accs))
        p = jnp.exp2(s_ref[slot] - m_new)
        if mask_chunk is not None:
            p = p * mask_chunk(c)
        p_ref[slot % 2] = p.astype(BF16)
        return m_new, cmaxes[1:] + (cmax_new,), accs

    p_ref[1] = jnp.zeros(p_ref.shape[1:], BF16)
    carry = (jnp.full((1, n), NEG_INF, F32),
             tuple(scores(min(i, last), i) for i in range(SWEEP_AHEAD)),
             tuple(jnp.zeros((VT_ROWS, wd), F32) for wd in widths))

    def trip(j, carry):
        for i in range(SWEEP_STEPS):
            carry = step(SWEEP_STEPS * j + i, i % SWEEP_SLOTS, carry)
        return carry

    looped = n_chunks // SWEEP_STEPS * SWEEP_STEPS
    carry = lax.fori_loop(0, n_chunks // SWEEP_STEPS, trip, carry)
    for c in range(looped, n_chunks):
        carry = step(c, c % SWEEP_SLOTS, carry)
    return values(last, last % 2, carry[2])


SWEEP_SLOTS = 4
SWEEP_AHEAD = 2
SWEEP_STEPS = 8


def _sweep_scratch(tk, n):
    return [pltpu.VMEM((SWEEP_SLOTS, tk, n), F32), pltpu.VMEM((P_SLOTS, tk, n), BF16)]


SAFE_LOG2_SPAN = 100.0
KMAX_ROWS = 1024


def _key_absmax(k_ref, kmax_ref, refresh):
    @pl.when(refresh)
    def _():
        def body(i, m):
            blk = k_ref[pl.ds(pl.multiple_of(i * KMAX_ROWS, KMAX_ROWS), KMAX_ROWS), :]
            return jnp.maximum(m, jnp.max(jnp.abs(blk.astype(F32)), axis=0, keepdims=True))
        m = lax.fori_loop(0, k_ref.shape[0] // KMAX_ROWS, body, jnp.zeros((1, LANES), F32))
        kmax_ref[...] = jnp.broadcast_to(m, kmax_ref.shape)


def _fixed_sweep(n_chunks, k_chunk, w, v_chunks, widths, s_ref, p_ref, m_ref, mask_chunk, first):
    offs = [sum(widths[:g]) for g in range(len(widths))]

    def chunk_of(t):
        return first if t == 0 else (t if t > first else t - 1)

    def probs(s, t):
        p = jnp.exp2(s - m_ref)
        if mask_chunk is not None:
            p = p * mask_chunk(chunk_of(t))
        p_ref[t % P_SLOTS] = p.astype(BF16)

    def values(t, accs):
        c = chunk_of(t)
        return tuple(
            acc + jnp.dot(v_chunks[g](c), p_ref[t % P_SLOTS, :, offs[g]:offs[g] + widths[g]],
                          preferred_element_type=F32)
            for g, acc in enumerate(accs))

    probs(s_ref[0], 0)
    accs = tuple(jnp.zeros((VT_ROWS, wd), F32) for wd in widths)

    for t in range(1, n_chunks):
        s = jnp.dot(k_chunk(chunk_of(t)), w, preferred_element_type=F32)
        if t >= PV_LAG:
            accs = values(t - PV_LAG, accs)
        probs(s, t)
    for t in range(max(n_chunks - PV_LAG, 0), n_chunks):
        accs = values(t, accs)
    return accs


PV_LAG = 3
P_SLOTS = PV_LAG + 1


def _softmax_sweep(n_chunks, k_chunk, w, v_chunks, widths, s_ref, p_ref, kmax_ref,
                   mask_chunk=None, first=0):
    s0 = jnp.dot(k_chunk(first), w, preferred_element_type=F32)
    s_ref[0] = s0
    if mask_chunk is not None:
        s0 = jnp.where(mask_chunk(first) > 0.0, s0, NEG_INF)
    m0 = jnp.max(s0, axis=0, keepdims=True)
    kmax = (kmax_ref[...] * 1.01).astype(BF16)
    bound = jnp.dot(kmax, jnp.abs(w), preferred_element_type=F32)[0:1]
    safe = jnp.max(bound - m0) <= SAFE_LOG2_SPAN
    return lax.cond(
        safe,
        lambda: _fixed_sweep(n_chunks, k_chunk, w, v_chunks, widths, s_ref, p_ref, m0,
                             mask_chunk, first),
        lambda: _flash_sweep(n_chunks, k_chunk, w, v_chunks, widths, s_ref, p_ref, mask_chunk))


def _diff_kernel(lam_ref, g_ref, qt_ref, k_ref, vt_ref, o_ref, s_ref, p_ref, kmax_ref, *,
                 lam_init, n_chunks):
    tq = qt_ref.shape[1]
    tk = vt_ref.shape[3]
    lp = lam_ref[...]
    lam = (jnp.exp(jnp.sum(lp[0:1] * lp[1:2], axis=1, keepdims=True))
           - jnp.exp(jnp.sum(lp[2:3] * lp[3:4], axis=1, keepdims=True)) + lam_init)

    qt = qt_ref[...].astype(F32)
    row = lax.broadcasted_iota(jnp.int32, (LANES, tq), 0)
    cols = [jnp.where((row >= DIFF_QK_DIM * c) & (row < DIFF_QK_DIM * (c + 1)), qt, 0.0)
            for c in range(4)]
    w = jnp.concatenate(cols, axis=1).astype(BF16)

    _key_absmax(k_ref, kmax_ref, pl.program_id(2) == 0)
    accs = _softmax_sweep(
        n_chunks,
        lambda c: k_ref[pl.ds(pl.multiple_of(c * tk, tk), tk), :],
        w,
        [lambda c: vt_ref[0, c], lambda c: vt_ref[1, c]],
        [2 * tq, 2 * tq], s_ref, p_ref, kmax_ref)

    gain = g_ref[...] * (1.0 - lam_init)
    for hl, acc in enumerate(accs):
        o1 = acc[0:HEAD_DIM, 0:tq] / acc[HEAD_DIM:HEAD_DIM + 1, 0:tq]
        o2 = acc[0:HEAD_DIM, tq:2 * tq] / acc[HEAD_DIM:HEAD_DIM + 1, tq:2 * tq]
        o = o1 - lam * o2
        ms = jnp.mean(o * o, axis=0, keepdims=True)
        y = o * lax.rsqrt(ms + RMS_EPS) * gain
        o_ref[HEAD_DIM * hl:HEAD_DIM * (hl + 1), :] = y.astype(BF16)


def _diff_attention(lam_params, subln_g, qt, k, vt, layer_idx):
    b, _, s = qt.shape
    tq, tk = DIFF_TQ, ATT_TK
    nck = s // tk
    lam_init = 0.8 - 0.6 * math.exp(-0.3 * layer_idx)
    kern = functools.partial(_diff_kernel, lam_init=lam_init, n_chunks=nck)
    return pl.pallas_call(
        kern,
        grid=(b, 2, s // tq),
        in_specs=[
            pl.BlockSpec((4, DIFF_QK_DIM), lambda bi, j, qi: (0, 0)),
            pl.BlockSpec((HEAD_DIM, 1), lambda bi, j, qi: (0, 0)),
            pl.BlockSpec((None, LANES, tq), lambda bi, j, qi: (bi, j, qi)),
            pl.BlockSpec((None, s, LANES), lambda bi, j, qi: (bi, 0, j)),
            pl.BlockSpec((None, 2, nck, VT_ROWS, tk), lambda bi, j, qi: (bi, j, 0, 0, 0)),
        ],
        out_specs=pl.BlockSpec((None, LANES, tq), lambda bi, j, qi: (bi, j, qi)),
        out_shape=jax.ShapeDtypeStruct((b, GROUP_WIDTH, s), BF16),
        scratch_shapes=_sweep_scratch(tk, 4 * tq) + [pltpu.VMEM((8, LANES), F32)],
        compiler_params=_cparams("arbitrary", "arbitrary", "arbitrary"),
        name="diff_attention",
    )(lam_params, subln_g, qt, k, vt)


def _mla_kernel(qt_ref, k_ref, vt_ref, o_ref, s_ref, p_ref, kmax_ref, *, n_chunks):
    tq = qt_ref.shape[1]
    tk = vt_ref.shape[2]
    _key_absmax(k_ref, kmax_ref, pl.program_id(2) == 0)
    (acc,) = _softmax_sweep(
        n_chunks,
        lambda c: k_ref[pl.ds(pl.multiple_of(c * tk, tk), tk), :],
        qt_ref[...],
        [lambda c: vt_ref[c]],
        [tq], s_ref, p_ref, kmax_ref)
    o_ref[...] = (acc[0:HEAD_DIM] / acc[HEAD_DIM:HEAD_DIM + 1]).astype(BF16)


def _mla_attention(qt, k, vt):
    b, _, s = qt.shape
    tq, tk = MLA_TQ, ATT_TK
    nck = s // tk
    return pl.pallas_call(
        functools.partial(_mla_kernel, n_chunks=nck),
        grid=(b, 4, s // tq),
        in_specs=[
            pl.BlockSpec((None, LANES, tq), lambda bi, h, qi: (bi, h, qi)),
            pl.BlockSpec((None, s, LANES), lambda bi, h, qi: (bi, 0, h)),
            pl.BlockSpec((None, None, nck, VT_ROWS, tk), lambda bi, h, qi: (bi, h, 0, 0, 0)),
        ],
        out_specs=pl.BlockSpec((None, HEAD_DIM, tq), lambda bi, h, qi: (bi, h, qi)),
        out_shape=jax.ShapeDtypeStruct((b, GROUP_WIDTH, s), BF16),
        scratch_shapes=_sweep_scratch(tk, tq) + [pltpu.VMEM((8, LANES), F32)],
        compiler_params=_cparams("arbitrary", "arbitrary", "arbitrary"),
        name="mla_attention",
    )(qt, k, vt)


def _dil_multiplicity(tq, tk):
    n_win = (tq + 2 * DIL_HALO) // tk
    key = jnp.arange(n_win * tk)[:, None] - DIL_HALO
    delta = key - jnp.arange(tq)[None, :]
    mult = jnp.zeros(delta.shape, F32)
    for window, dil in DIL_PATTERNS:
        reach = (window // 2 // dil) * dil
        mult = mult + ((delta % dil == 0) & (jnp.abs(delta) <= reach)).astype(F32)
    return mult.reshape(n_win, tk, tq)


def _dil_kernel(c_ref, qt_ref, k_ref, vt_ref, o_ref, s_ref, p_ref, kmax_ref, *, n_seq_chunks):
    tq = qt_ref.shape[1]
    n_win, tk = c_ref.shape[0], c_ref.shape[1]
    halo_chunks = DIL_HALO // tk
    qi = pl.program_id(2)

    qt = qt_ref[...].astype(F32)
    row = lax.broadcasted_iota(jnp.int32, (LANES, tq), 0)
    w = jnp.concatenate([jnp.where(row < HEAD_DIM, qt, 0.0),
                         jnp.where(row >= HEAD_DIM, qt, 0.0)], axis=1).astype(BF16)

    def seq_chunk(c):
        return qi * (tq // tk) + c - halo_chunks

    def clamped(c):
        return jnp.clip(seq_chunk(c), 0, n_seq_chunks - 1)

    def mask_chunk(c):
        ck = seq_chunk(c)
        inside = jnp.where((ck >= 0) & (ck < n_seq_chunks), 1.0, 0.0)
        mult = c_ref[c] * inside
        return jnp.concatenate([mult, mult], axis=1)

    def k_chunk(c):
        return k_ref[pl.ds(pl.multiple_of(clamped(c) * tk, tk), tk), :]

    _key_absmax(k_ref, kmax_ref, qi == 0)
    accs = _softmax_sweep(
        n_win, k_chunk, w,
        [lambda c: vt_ref[0, clamped(c)], lambda c: vt_ref[1, clamped(c)]],
        [tq, tq], s_ref, p_ref, kmax_ref, mask_chunk=mask_chunk, first=n_win // 2)
    for hl, acc in enumerate(accs):
        o_ref[HEAD_DIM * hl:HEAD_DIM * (hl + 1), :] = (
            acc[0:HEAD_DIM] / acc[HEAD_DIM:HEAD_DIM + 1]).astype(BF16)


def _dil_attention(mult, qt, k, vt):
    b, _, s = qt.shape
    tq, tk = DIL_TQ, DIL_TK
    n_win = mult.shape[0]
    return pl.pallas_call(
        functools.partial(_dil_kernel, n_seq_chunks=s // tk),
        grid=(b, 2, s // tq),
        in_specs=[
            pl.BlockSpec((n_win, tk, tq), lambda bi, j, qi: (0, 0, 0)),
            pl.BlockSpec((None, LANES, tq), lambda bi, j, qi: (bi, j, qi)),
            pl.BlockSpec((None, s, LANES), lambda bi, j, qi: (bi, 0, j)),
            pl.BlockSpec((None, 2, s // tk, VT_ROWS, tk), lambda bi, j, qi: (bi, j, 0, 0, 0)),
        ],
        out_specs=pl.BlockSpec((None, LANES, tq), lambda bi, j, qi: (bi, j, qi)),
        out_shape=jax.ShapeDtypeStruct((b, GROUP_WIDTH, s), BF16),
        scratch_shapes=_sweep_scratch(tk, 2 * tq) + [pltpu.VMEM((8, LANES), F32)],
        compiler_params=_cparams("arbitrary", "arbitrary", "arbitrary"),
        name="dilated_attention",
    )(mult, qt, k, vt)


def _with_halo(prev_ref, cur_ref, next_ref, i, n_tiles):
    prev = jnp.where(i > 0, prev_ref[...], 0.0)
    nxt = jnp.where(i < n_tiles - 1, next_ref[...], 0.0)
    return jnp.concatenate([prev, cur_ref[...], nxt], axis=0)


def _shift_rows(x, k):
    n = x.shape[0]
    return pltpu.roll(x, (n - k) % n, 0)


def _pool_kernel(prev_ref, cur_ref, next_ref, w_ref, scale_ref, o_ref, *, seq_len):
    tm = cur_ref.shape[0]
    i = pl.program_id(1)
    x = _with_halo(prev_ref, cur_ref, next_ref, i, seq_len // tm)
    t = i * tm + lax.broadcasted_iota(jnp.int32, (tm, 1), 0)
    lane_group = lax.broadcasted_iota(jnp.int32, (tm, GROUP_WIDTH), 1) // POOL_GROUP

    run = x
    mean = None
    for g, wnd in enumerate(POOL_WINDOWS):
        run = run + _shift_rows(run, wnd // 2)
        total = _shift_rows(run, -(wnd // 2))[HALO:HALO + tm]
        lo = jnp.clip(t - wnd // 2, 0, seq_len - 1)
        hi = jnp.clip(t + wnd - wnd // 2 - 1, 0, seq_len - 1)
        cand = total / (hi - lo + 1).astype(F32)
        mean = cand if mean is None else jnp.where(lane_group == g, cand, mean)
    d = (mean - cur_ref[...]).astype(BF16)
    y = jnp.dot(d, w_ref[...], preferred_element_type=F32) * scale_ref[...]
    o_ref[...] = y.astype(BF16)


def _halo_specs(tm, cols, seq_len):
    nb = tm // HALO
    last = seq_len // HALO - 1
    return [
        pl.BlockSpec((None, HALO, cols), lambda bi, i: (bi, jnp.maximum(i * nb - 1, 0), 0)),
        pl.BlockSpec((None, tm, cols), lambda bi, i: (bi, i, 0)),
        pl.BlockSpec((None, HALO, cols), lambda bi, i: (bi, jnp.minimum((i + 1) * nb, last), 0)),
    ]


def _pool_mixer(hc3, w_bd, scale):
    b, s, c = hc3.shape
    tm = POOL_TM
    prev_s, cur_s, next_s = _halo_specs(tm, c, s)
    return pl.pallas_call(
        functools.partial(_pool_kernel, seq_len=s),
        grid=(b, s // tm),
        in_specs=[prev_s, cur_s, next_s,
                  pl.BlockSpec(w_bd.shape, lambda bi, i: (0, 0)),
                  pl.BlockSpec(scale.shape, lambda bi, i: (0, 0))],
        out_specs=pl.BlockSpec((None, tm, c), lambda bi, i: (bi, i, 0)),
        out_shape=jax.ShapeDtypeStruct((b, s, c), BF16),
        compiler_params=_cparams("arbitrary", "arbitrary"),
        name="pool_mixer",
    )(hc3, hc3, hc3, w_bd, scale)


def _layer_norm(z, g, b):
    mu = jnp.mean(z, axis=-1, keepdims=True)
    zc = z - mu
    var = jnp.mean(zc * zc, axis=-1, keepdims=True)
    return zc * lax.rsqrt(var + LN_EPS) * g + b


def _tdot(at, w):
    return lax.dot_general(at, w, (((0,), (0,)), ((), ())), preferred_element_type=F32)


def _out_kernel(x_ref, ya_ref, yb_ref, yc_ref, yd_ref, w_ref, g_ref, b_ref, o_ref):
    gw = GROUP_WIDTH
    z = _tdot(ya_ref[...], w_ref[0:gw, :])
    z = z + _tdot(yb_ref[...], w_ref[gw:2 * gw, :])
    z = z + jnp.dot(yc_ref[...], w_ref[2 * gw:3 * gw, :], preferred_element_type=F32)
    z = z + _tdot(yd_ref[...], w_ref[3 * gw:4 * gw, :])
    o_ref[...] = _layer_norm(DN_ALPHA * x_ref[...] + z, g_ref[...], b_ref[...])


def _out_proj(x3, ya_t, yb_t, yc, yd_t, w, g, bias):
    b, s, d = x3.shape
    tm = OUT_TM
    tok = lambda n: pl.BlockSpec((None, tm, n), lambda bi, i: (bi, i, 0))
    feat = pl.BlockSpec((None, GROUP_WIDTH, tm), lambda bi, i: (bi, 0, i))
    full = lambda a: pl.BlockSpec(a.shape, lambda bi, i: (0,) * a.ndim)
    return pl.pallas_call(
        _out_kernel,
        grid=(b, s // tm),
        in_specs=[tok(d), feat, feat, tok(GROUP_WIDTH), feat, full(w), full(g), full(bias)],
        out_specs=tok(d),
        out_shape=jax.ShapeDtypeStruct((b, s, d), F32),
        compiler_params=_cparams("arbitrary", "arbitrary"),
        name="out_proj_ln",
    )(x3, ya_t, yb_t, yc, yd_t, w, g, bias)


def _ffn_kernel(prev_ref, cur_ref, next_ref, wup_ref, wd_ref, cw_ref, g_ref, b_ref, o_ref,
                act_ref, *, seq_len):
    tm = cur_ref.shape[0]
    i = pl.program_id(1)
    xe = _with_halo(prev_ref, cur_ref, next_ref, i, seq_len // tm).astype(BF16)

    def conv_proj(j0):
        u = jnp.dot(xe, wup_ref[:, j0:j0 + FFN_CHUNK], preferred_element_type=F32)
        c = cw_ref[:, j0:j0 + FFN_CHUNK]
        y = _shift_rows(u, -1) * c[0:1] + u * c[1:2] + _shift_rows(u, 1) * c[2:3] + c[3:4]
        return y[HALO:HALO + tm]

    for j0 in range(0, D_FF, FFN_CHUNK):
        gate = conv_proj(j0)
        up = conv_proj(D_FF + j0)
        act_ref[:, j0:j0 + FFN_CHUNK] = (gate * jax.nn.sigmoid(gate) * up).astype(BF16)
    f = jnp.dot(act_ref[...], wd_ref[...], preferred_element_type=F32)
    o_ref[...] = _layer_norm(DN_ALPHA * cur_ref[...] + f, g_ref[...], b_ref[...])


def _ffn(x3, wup, wd, cw, g, bias):
    b, s, d = x3.shape
    tm = FFN_TM
    full = lambda a: pl.BlockSpec(a.shape, lambda bi, i: (0,) * a.ndim)
    return pl.pallas_call(
        functools.partial(_ffn_kernel, seq_len=s),
        grid=(b, s // tm),
        in_specs=_halo_specs(tm, d, s) + [full(wup), full(wd), full(cw), full(g), full(bias)],
        out_specs=pl.BlockSpec((None, tm, d), lambda bi, i: (bi, i, 0)),
        out_shape=jax.ShapeDtypeStruct((b, s, d), F32),
        scratch_shapes=[pltpu.VMEM((tm, D_FF), BF16)],
        compiler_params=_cparams("arbitrary", "arbitrary"),
        name="conv_ffn_ln",
    )(x3, x3, x3, wup, wd, cw, g, bias)


def _pad_cols(w, groups, width, padded):
    k = w.shape[0]
    w = w.reshape(k, groups, width)
    return jnp.pad(w, ((0, 0), (0, 0), (0, padded - width))).reshape(k, groups * padded)


def kernel(x, positions, w_in, diff_lambda, diff_subln, pool_w, pool_scale, mla_q_norm,
           mla_kv_norm, mla_w_uq, mla_w_ukv, w_out, ln1_g, ln1_b, ffn_w_up, ffn_conv_w,
           ffn_conv_b, ffn_w_down, ln2_g, ln2_b):
    b, s, d = x.shape
    t = b * s
    assert d == D_MODEL and s % 1024 == 0

    consts = jnp.stack([
        _rope_consts(DIFF_QK_DIM, DIFF_QK_DIM // 4, 0),
        _rope_consts(HEAD_DIM, HEAD_DIM // 4, 0),
        _rope_consts(LANES, MLA_ROPE_DIM, MLA_NOPE_DIM),
    ])
    tables = _rope_tables(positions.reshape(t, 1), consts)
    ta, tb, td = tables[0], tables[1], tables[2]
    mult = _dil_multiplicity(DIL_TQ, DIL_TK)

    for l in range(DEPTH):
        wi = w_in[l]
        w_proj = jnp.concatenate([
            wi[:, :7 * GROUP_WIDTH + MLA_Q_RANK + MLA_KV_RANK],
            jnp.zeros((d, MLA_NOPE_DIM), wi.dtype),
            wi[:, 7 * GROUP_WIDTH + MLA_Q_RANK + MLA_KV_RANK:],
            jnp.zeros((d, LANES - MLA_NOPE_DIM - MLA_ROPE_DIM), wi.dtype)], axis=1).astype(BF16)
        wq = _pad_cols(mla_w_uq[l], 4, MLA_NOPE_DIM + MLA_ROPE_DIM, LANES).astype(BF16)
        wkv = mla_w_ukv[l].reshape(MLA_KV_RANK, 4, 2, HEAD_DIM)
        wk = _pad_cols(wkv[:, :, 0].reshape(MLA_KV_RANK, 4 * HEAD_DIM), 4, HEAD_DIM, LANES).astype(BF16)
        wv = wkv[:, :, 1].reshape(MLA_KV_RANK, 4 * HEAD_DIM).astype(BF16)
        pw = pool_w[l]
        w_pool = jnp.zeros((GROUP_WIDTH, GROUP_WIDTH), pw.dtype)
        for g in range(4):
            w_pool = w_pool.at[g * POOL_GROUP:(g + 1) * POOL_GROUP,
                               g * POOL_GROUP:(g + 1) * POOL_GROUP].set(pw[g])
        w_pool = w_pool.astype(BF16)
        conv = jnp.concatenate([ffn_conv_w[l], ffn_conv_b[l][None, :],
                                jnp.zeros((4, 2 * D_FF), F32)], axis=0)

        qa_t, ka, va_t, qb_t, kb, vb_t, hc, hd = _proj(x, w_proj, ta, tb)
        qd_t, kd, vd_t = _mla_prep(hd, td, mla_q_norm[l][None, :], mla_kv_norm[l][None, :],
                                   wq, wk, wv)
        ya_t = _diff_attention(diff_lambda[l], diff_subln[l][:, None], qa_t, ka, va_t, l)
        yb_t = _dil_attention(mult, qb_t, kb, vb_t)
        yc = _pool_mixer(hc, w_pool, pool_scale[l][None, :])
        yd_t = _mla_attention(qd_t, kd, vd_t)

        x = _out_proj(x, ya_t, yb_t, yc, yd_t, w_out[l].astype(BF16),
                      ln1_g[l][None, :], ln1_b[l][None, :])
        x = _ffn(x, ffn_w_up[l].astype(BF16), ffn_w_down[l].astype(BF16), conv,
                 ln2_g[l][None, :], ln2_b[l][None, :])
    return x
```
